```python
import jax, jax.numpy as jnp
from jax import lax
import numpy as np

D_MODEL = 1024
BATCH = 32
SEQ = 2048
DEPTH = 1

RET_HEADS = 4
RET_DK = 128
RET_DV = 256
RET_CHUNK = 128
FOX_HEADS = 8
FOX_DH = 64
FOX_BLOCK = 128
PEER_HEADS = 8
PEER_NKEYS = 128
PEER_NEXP = PEER_NKEYS * PEER_NKEYS
PEER_DQ = 256
PEER_HALF = PEER_DQ // 2
PEER_TOPK = 16
PEER_TOK_BLOCK = 128
PLE_DIM = 256
LN_EPS = 1e-5
ROPE_BASE = 10000.0
ALPHA = (2.0 * DEPTH) ** 0.25
BETA = (8.0 * DEPTH) ** -0.25

RET_QK_W = RET_HEADS * RET_DK
RET_V_W = RET_HEADS * RET_DV
FOX_W = FOX_HEADS * FOX_DH
IN_SPLITS = (RET_QK_W, RET_QK_W, RET_V_W, RET_V_W, FOX_W, FOX_W, FOX_W, FOX_HEADS, D_MODEL, D_MODEL)
N_IN = sum(IN_SPLITS)

kernel_name = "hybrid_retention_fox_peer_deepnorm"


def layer_norm(x, g, b):
    xf = x.astype(jnp.float32)
    mu = jnp.mean(xf, axis=-1, keepdims=True)
    var = jnp.mean(jnp.square(xf - mu), axis=-1, keepdims=True)
    y = (xf - mu) * lax.rsqrt(var + LN_EPS)
    return (y * g.astype(jnp.float32) + b.astype(jnp.float32)).astype(x.dtype)


def group_norm_heads(y):
    mu = jnp.mean(y, axis=-1, keepdims=True)
    var = jnp.mean(jnp.square(y - mu), axis=-1, keepdims=True)
    return (y - mu) * lax.rsqrt(var + LN_EPS)


def rotary(x, pos):
    d = x.shape[-1]
    half = d // 2
    inv = ROPE_BASE ** (-jnp.arange(half, dtype=jnp.float32) / half)
    ang = pos.astype(jnp.float32)[:, None] * inv[None, :]
    cos = jnp.cos(ang)[None, :, None, :]
    sin = jnp.sin(ang)[None, :, None, :]
    xf = x.astype(jnp.float32)
    x1, x2 = xf[..., :half], xf[..., half:]
    return jnp.concatenate([x1 * cos - x2 * sin, x2 * cos + x1 * sin], axis=-1)


def retention(q, k, v):
    B, S, H, dk = q.shape
    dv = v.shape[-1]
    C = RET_CHUNK
    n = S // C
    log_g = jnp.log(1.0 - 2.0 ** (-5.0 - jnp.arange(H, dtype=jnp.float32)))
    idx = jnp.arange(C, dtype=jnp.float32)
    diff = idx[:, None] - idx[None, :]
    intra = jnp.where(diff >= 0, jnp.exp(log_g[:, None, None] * jnp.maximum(diff, 0.0)), 0.0)
    q_decay = jnp.exp(log_g[:, None] * (idx + 1.0))[..., None]
    k_decay = jnp.exp(log_g[:, None] * (C - 1.0 - idx))[..., None]
    chunk_decay = jnp.exp(log_g * C)[:, None, None]

    def to_chunks(t):
        return t.reshape(B, n, C, H, t.shape[-1]).transpose(1, 0, 3, 2, 4)

    qc, kc, vc = to_chunks(q), to_chunks(k), to_chunks(v)

    def step(R, inp):
        qi, ki, vi = inp
        s = jnp.einsum('bhid,bhjd->bhij', qi, ki) * intra
        inner = jnp.einsum('bhij,bhjv->bhiv', s, vi)
        cross = jnp.einsum('bhid,bhdv->bhiv', qi * q_decay, R)
        R_new = chunk_decay * R + jnp.einsum('bhjd,bhjv->bhdv', ki * k_decay, vi)
        return R_new, inner + cross

    R0 = jnp.zeros((B, H, dk, dv), jnp.float32)
    _, out = lax.scan(step, R0, (qc, kc, vc))
    return out.transpose(1, 0, 3, 2, 4).reshape(B, S, H, dv)


def forgetting_attention(q, k, v, log_f):
    B, S, H, d = q.shape
    nb = S // FOX_BLOCK
    c = jnp.cumsum(log_f, axis=1).transpose(0, 2, 1)
    qh = q.transpose(0, 2, 1, 3)
    kh = k.transpose(0, 2, 1, 3)
    vh = v.transpose(0, 2, 1, 3)
    qb = qh.reshape(B, H, nb, FOX_BLOCK, d).transpose(2, 0, 1, 3, 4)
    cb = c.reshape(B, H, nb, FOX_BLOCK).transpose(2, 0, 1, 3)
    starts = jnp.arange(nb, dtype=jnp.int32) * FOX_BLOCK
    kpos = jnp.arange(S, dtype=jnp.int32)
    scale = d ** -0.5

    def block(inp):
        qi, ci, start = inp
        qpos = start + jnp.arange(FOX_BLOCK, dtype=jnp.int32)
        logits = jnp.einsum('bhqd,bhkd->bhqk', qi, kh).astype(jnp.float32) * scale
        logits = logits + ci[..., None] - c[:, :, None, :]
        logits = jnp.where(kpos[None, :] <= qpos[:, None], logits, -1e30)
        probs = jax.nn.softmax(logits, axis=-1)
        return jnp.einsum('bhqk,bhkd->bhqd', probs.astype(vh.dtype), vh)

    out = lax.map(block, (qb, cb, starts))
    return out.transpose(1, 0, 3, 2, 4).reshape(B, S, H * d)


def peer(x, w_q, sub_keys, expert_u, expert_v):
    B, S, D = x.shape
    nt = (B * S) // PEER_TOK_BLOCK
    xt = x.reshape(nt, PEER_TOK_BLOCK, D)

    def block(xc):
        n = xc.shape[0]
        q = (xc @ w_q).reshape(n, PEER_HEADS, 2, PEER_HALF)
        sc = jnp.einsum('nhcd,hckd->nhck', q, sub_keys).astype(jnp.float32)
        s, idx = lax.top_k(sc, PEER_TOPK)
        cand = (s[:, :, 0, :, None] + s[:, :, 1, None, :]).reshape(n, PEER_HEADS, PEER_TOPK * PEER_TOPK)
        cid = (idx[:, :, 0, :, None] * PEER_NKEYS + idx[:, :, 1, None, :]).reshape(n, PEER_HEADS, PEER_TOPK * PEER_TOPK)
        top_s, pos = lax.top_k(cand, PEER_TOPK)
        eid = jnp.take_along_axis(cid, pos, axis=-1)
        gate = jax.nn.softmax(top_s, axis=-1)
        u = expert_u[eid]
        hid = jnp.einsum('nhkd,nd->nhk', u, xc).astype(jnp.float32)
        act = (jax.nn.gelu(hid, approximate=False) * gate).astype(xc.dtype)
        return jnp.einsum('nhk,nhkd->nd', act, expert_v[eid])

    return lax.map(block, xt).reshape(B, S, D)


def setup_inputs(seed: int = 0) -> dict:
    key = jax.random.key(seed)
    ks = jax.random.split(key, 24)
    f32 = jnp.float32
    D = D_MODEL
    nrm = lambda k, shape, s: jax.random.normal(k, shape, f32) * s
    return {
        "x": nrm(ks[0], (BATCH, SEQ, D), 1.0),
        "p": nrm(ks[1], (DEPTH, BATCH, SEQ, PLE_DIM), 1.0),
        "ln_emb_g": 1.0 + nrm(ks[2], (D,), 0.02),
        "ln_emb_b": nrm(ks[3], (D,), 0.02),
        "w_in": nrm(ks[4], (DEPTH, D, N_IN), D ** -0.5),
        "b_forget": 2.0 + nrm(ks[5], (DEPTH, FOX_HEADS), 0.5),
        "b_branch_gate": nrm(ks[6], (DEPTH, 2, D), 0.02),
        "w_ret_o": nrm(ks[7], (DEPTH, RET_V_W, D), RET_V_W ** -0.5),
        "w_fox_o": nrm(ks[8], (DEPTH, FOX_W, D), FOX_W ** -0.5),
        "w_out": nrm(ks[9], (DEPTH, D, D), BETA * D ** -0.5),
        "ln1_g": 1.0 + nrm(ks[10], (DEPTH, D), 0.02),
        "ln1_b": nrm(ks[11], (DEPTH, D), 0.02),
        "w_peer_q": nrm(ks[12], (DEPTH, D, PEER_HEADS * PEER_DQ), D ** -0.5),
        "peer_sub_keys": nrm(ks[13], (DEPTH, PEER_HEADS, 2, PEER_NKEYS, PEER_HALF), PEER_HALF ** -0.5),
        "peer_u": nrm(ks[14], (DEPTH, PEER_NEXP, D), D ** -0.5),
        "peer_v": nrm(ks[15], (DEPTH, PEER_NEXP, D), BETA * PEER_HEADS ** -0.5),
        "w_ple_gate": nrm(ks[16], (DEPTH, D, D), D ** -0.5),
        "b_ple_gate": nrm(ks[17], (DEPTH, D), 0.02),
        "w_ple": nrm(ks[18], (DEPTH, PLE_DIM, D), BETA * PLE_DIM ** -0.5),
        "ln2_g": 1.0 + nrm(ks[19], (DEPTH, D), 0.02),
        "ln2_b": nrm(ks[20], (DEPTH, D), 0.02),
    }


def reference(x, p, ln_emb_g, ln_emb_b, w_in, b_forget, b_branch_gate, w_ret_o, w_fox_o, w_out,
              ln1_g, ln1_b, w_peer_q, peer_sub_keys, peer_u, peer_v, w_ple_gate, b_ple_gate, w_ple,
              ln2_g, ln2_b):
    B, S, _ = x.shape
    pos = jnp.arange(S, dtype=jnp.int32)
    split_points = np.cumsum(np.array(IN_SPLITS))[:-1].tolist()
    h = layer_norm(x, ln_emb_g, ln_emb_b)
    for i in range(DEPTH):
        proj = h @ w_in[i]
        rq, rk, rv, rg, fq, fk, fv, ff, gr, gf = jnp.split(proj, split_points, axis=-1)
        q_r = rotary(rq.reshape(B, S, RET_HEADS, RET_DK), pos)
        k_r = rotary(rk.reshape(B, S, RET_HEADS, RET_DK), pos) * (RET_DK ** -0.5)
        v_r = rv.reshape(B, S, RET_HEADS, RET_DV).astype(jnp.float32)
        y_r = group_norm_heads(retention(q_r, k_r, v_r)).reshape(B, S, RET_V_W).astype(h.dtype)
        y_ret = (jax.nn.silu(rg) * y_r) @ w_ret_o[i]
        log_f = jax.nn.log_sigmoid(ff.astype(jnp.float32) + b_forget[i].astype(jnp.float32))
        y_f = forgetting_attention(fq.reshape(B, S, FOX_HEADS, FOX_DH),
                                   fk.reshape(B, S, FOX_HEADS, FOX_DH),
                                   fv.reshape(B, S, FOX_HEADS, FOX_DH), log_f)
        y_fox = y_f.astype(h.dtype) @ w_fox_o[i]
        merged = jax.nn.sigmoid(gr + b_branch_gate[i, 0]) * y_ret + jax.nn.sigmoid(gf + b_branch_gate[i, 1]) * y_fox
        h = layer_norm(ALPHA * h + merged @ w_out[i], ln1_g[i], ln1_b[i])
        ple = jax.nn.sigmoid(h @ w_ple_gate[i] + b_ple_gate[i]) * (p[i] @ w_ple[i])
        ch = peer(h, w_peer_q[i], peer_sub_keys[i], peer_u[i], peer_v[i]) + ple
        h = layer_norm(ALPHA * h + ch, ln2_g[i], ln2_b[i])
    return h
```

```python
import functools
import math

import jax
import jax.numpy as jnp
from jax import lax
from jax.experimental import pallas as pl
from jax.experimental.pallas import tpu as pltpu

F32 = jnp.float32
BF16 = jnp.bfloat16

D_MODEL = 1024
RET_HEADS = 4
RET_DK = 128
RET_DV = 256
RET_CHUNK = 128
FOX_HEADS = 8
FOX_DH = 64
PEER_HEADS = 8
PEER_NKEYS = 128
PEER_NEXP = PEER_NKEYS * PEER_NKEYS
PEER_HALF = 128
PEER_TOPK = 16
PLE_DIM = 256
LN_EPS = 1e-5
ROPE_BASE = 10000.0

RET_QK_W = RET_HEADS * RET_DK
RET_V_W = RET_HEADS * RET_DV
FOX_W = FOX_HEADS * FOX_DH

OFF_RQ, OFF_RK, OFF_RV, OFF_RG = 0, 512, 1024, 2048
OFF_GR, OFF_GF = 3072, 4096
OFF_FQ, OFF_FK, OFF_FV = 5120, 5632, 6144
PROJ_W = 6656
FF_W = 512

V7X_VMEM_LIMIT = 56 * 1024 * 1024
NEG_BIG = -3.0e38
CAND_CELLS = tuple((a, b) for a in range(PEER_TOPK) for b in range(PEER_TOPK)
                   if (a + 1) * (b + 1) <= PEER_TOPK)


def _layer_norm(x, g, b):
    mu = jnp.mean(x, axis=-1, keepdims=True)
    xc = x - mu
    var = jnp.mean(xc * xc, axis=-1, keepdims=True)
    return xc * lax.rsqrt(var + LN_EPS) * g + b


def _params(*sem):
    return pltpu.CompilerParams(dimension_semantics=sem, vmem_limit_bytes=V7X_VMEM_LIMIT)


def _inproj_kernel(x_ref, g_ref, b_ref, w_ref, wff_ref, o_ref, ff_ref, h_scr):
    @pl.when(pl.program_id(1) == 0)
    def _():
        hb = _layer_norm(x_ref[...], g_ref[...], b_ref[...]).astype(BF16)
        h_scr[...] = hb
        ff_ref[...] = jnp.dot(hb, wff_ref[...], preferred_element_type=F32)

    o_ref[...] = jnp.dot(h_scr[...], w_ref[...], preferred_element_type=F32).astype(BF16)


def _in_proj(x2, g, b, w_main, w_ff, tm=512, tn=1664):
    t = x2.shape[0]
    return pl.pallas_call(
        _inproj_kernel,
        grid=(t // tm, PROJ_W // tn),
        in_specs=[
            pl.BlockSpec((tm, D_MODEL), lambda i, j: (i, 0)),
            pl.BlockSpec((1, D_MODEL), lambda i, j: (0, 0)),
            pl.BlockSpec((1, D_MODEL), lambda i, j: (0, 0)),
            pl.BlockSpec((D_MODEL, tn), lambda i, j: (0, j)),
            pl.BlockSpec((D_MODEL, FF_W), lambda i, j: (0, 0)),
        ],
        out_specs=[
            pl.BlockSpec((tm, tn), lambda i, j: (i, j)),
            pl.BlockSpec((tm, FF_W), lambda i, j: (i, 0)),
        ],
        out_shape=[jax.ShapeDtypeStruct((t, PROJ_W), BF16), jax.ShapeDtypeStruct((t, FF_W), F32)],
        scratch_shapes=[pltpu.VMEM((tm, D_MODEL), BF16)],
        compiler_params=_params("parallel", "arbitrary"),
        name="in_proj",
    )(x2, g, b, w_main, w_ff)


def _retention_kernel(q_ref, k_ref, v_ref, g_ref, cos_ref, sin_ref, intra_ref, qd_ref, kd_ref,
                      o_ref, r_scr, *, chunk_decay):
    @pl.when(pl.program_id(1) == 0)
    def _():
        r_scr[...] = jnp.zeros_like(r_scr)

    cos = cos_ref[...]
    sin = sin_ref[...]
    for h in range(RET_HEADS):
        q = q_ref[:, h * RET_DK:(h + 1) * RET_DK].astype(F32)
        k = k_ref[:, h * RET_DK:(h + 1) * RET_DK].astype(F32)
        v = v_ref[:, h * RET_DV:(h + 1) * RET_DV]
        qr = q * cos + pltpu.roll(q, RET_DK // 2, 1) * sin
        kr = (k * cos + pltpu.roll(k, RET_DK // 2, 1) * sin) * (RET_DK ** -0.5)
        s = lax.dot_general(qr.astype(BF16), kr.astype(BF16), (((1,), (1,)), ((), ())),
                            preferred_element_type=F32) * intra_ref[h]
        inner = jnp.dot(s.astype(BF16), v, preferred_element_type=F32)
        r_old = r_scr[h]
        cross = jnp.dot((qr * qd_ref[h]).astype(BF16), r_old.astype(BF16), preferred_element_type=F32)
        kv = lax.dot_general((kr * kd_ref[h]).astype(BF16), v, (((0,), (0,)), ((), ())),
                             preferred_element_type=F32)
        r_scr[h] = chunk_decay[h] * r_old + kv
        y = inner + cross
        mu = jnp.mean(y, axis=-1, keepdims=True)
        yc = y - mu
        var = jnp.mean(yc * yc, axis=-1, keepdims=True)
        yn = yc * lax.rsqrt(var + LN_EPS)
        gate = g_ref[:, h * RET_DV:(h + 1) * RET_DV].astype(F32)
        o_ref[:, h * RET_DV:(h + 1) * RET_DV] = (gate * jax.nn.sigmoid(gate) * yn).astype(BF16)


def _retention(proj, cos_t, sin_t, intra, qdec, kdec, chunk_decay, batch, seq):
    c = RET_CHUNK
    nc = seq // c
    row = lambda b, j: b * nc + j
    return pl.pallas_call(
        functools.partial(_retention_kernel, chunk_decay=chunk_decay),
        grid=(batch, nc),
        in_specs=[
            pl.BlockSpec((c, RET_QK_W), lambda b, j: (row(b, j), OFF_RQ // RET_QK_W)),
            pl.BlockSpec((c, RET_QK_W), lambda b, j: (row(b, j), OFF_RK // RET_QK_W)),
            pl.BlockSpec((c, RET_V_W), lambda b, j: (row(b, j), OFF_RV // RET_V_W)),
            pl.BlockSpec((c, RET_V_W), lambda b, j: (row(b, j), OFF_RG // RET_V_W)),
            pl.BlockSpec((c, RET_DK), lambda b, j: (j, 0)),
            pl.BlockSpec((c, RET_DK), lambda b, j: (j, 0)),
            pl.BlockSpec((RET_HEADS, c, c), lambda b, j: (0, 0, 0)),
            pl.BlockSpec((RET_HEADS, c, RET_DK), lambda b, j: (0, 0, 0)),
            pl.BlockSpec((RET_HEADS, c, RET_DK), lambda b, j: (0, 0, 0)),
        ],
        out_specs=pl.BlockSpec((c, RET_V_W), lambda b, j: (row(b, j), 0)),
        out_shape=jax.ShapeDtypeStruct((batch * seq, RET_V_W), BF16),
        scratch_shapes=[pltpu.VMEM((RET_HEADS, RET_DK, RET_DV), F32)],
        compiler_params=_params("parallel", "arbitrary"),
        name="retention",
    )(proj, proj, proj, proj, cos_t, sin_t, intra, qdec, kdec)


def _fcumsum_kernel(ff_ref, bf_ref, tri_ref, c_ref, ct_ref, *, blk):
    seq = ff_ref.shape[0]
    carry = jnp.zeros((1, FF_W), F32)
    for i in range(seq // blk):
        lf = jax.nn.log_sigmoid(ff_ref[i * blk:(i + 1) * blk, :] + bf_ref[...])
        cs = jnp.dot(tri_ref[...], lf, preferred_element_type=F32,
                     precision=lax.Precision.HIGHEST) + carry
        c_ref[i * blk:(i + 1) * blk, :] = cs
        carry = cs[blk - 1:blk, :]
        for p in range(FOX_HEADS // 2):
            ct_ref[0, p, i] = cs[:, p * 128:(p + 1) * 128].T[0:8, :]


def _fcumsum(ff, bf_pad, batch, seq, blk):
    tri = (lax.broadcasted_iota(jnp.int32, (blk, blk), 0)
           >= lax.broadcasted_iota(jnp.int32, (blk, blk), 1)).astype(F32)
    nb = seq // blk
    return pl.pallas_call(
        functools.partial(_fcumsum_kernel, blk=blk),
        grid=(batch,),
        in_specs=[
            pl.BlockSpec((seq, FF_W), lambda b: (b, 0)),
            pl.BlockSpec((1, FF_W), lambda b: (0, 0)),
            pl.BlockSpec((blk, blk), lambda b: (0, 0)),
        ],
        out_specs=[
            pl.BlockSpec((seq, FF_W), lambda b: (b, 0)),
            pl.BlockSpec((1, FOX_HEADS // 2, nb, 8, blk), lambda b: (b, 0, 0, 0, 0)),
        ],
        out_shape=[jax.ShapeDtypeStruct((batch * seq, FF_W), F32),
                   jax.ShapeDtypeStruct((batch, FOX_HEADS // 2, nb, 8, blk), F32)],
        compiler_params=_params("parallel"),
        name="fcumsum",
    )(ff, bf_pad, tri)


def _fox_kernel(q_ref, k_ref, v_ref, c_ref, ct_ref, o_ref, m_scr, l_scr, acc_scr, *, tq):
    qi = pl.program_id(2)
    lane = lax.broadcasted_iota(jnp.int32, (tq, 128), 1)
    rows = lax.broadcasted_iota(jnp.int32, (tq, tq), 0)
    cols = lax.broadcasted_iota(jnp.int32, (tq, tq), 1)
    q = q_ref[...]
    outs = []
    for r in range(2):
        in_head = (lane >= r * FOX_DH) & (lane < (r + 1) * FOX_DH)
        qh = jnp.where(in_head, q, jnp.zeros_like(q))
        cq = c_ref[:, r:r + 1]
        m_scr[...] = jnp.full_like(m_scr, NEG_BIG)
        l_scr[...] = jnp.zeros_like(l_scr)
        acc_scr[...] = jnp.zeros_like(acc_scr)

        def block(j, diag, qh=qh, cq=cq, r=r):
            start = pl.multiple_of(j * tq, tq)
            kb = k_ref[pl.ds(start, tq), :]
            vb = v_ref[pl.ds(start, tq), :]
            ck = ct_ref[0, 0, j, r:r + 1, :]
            s = lax.dot_general(qh, kb, (((1,), (1,)), ((), ())), preferred_element_type=F32)
            s = s * (FOX_DH ** -0.5) + cq - ck
            if diag:
                s = jnp.where(cols <= rows, s, -1e30)
            m_old = m_scr[...]
            m_new = jnp.maximum(m_old, jnp.max(s, axis=-1, keepdims=True))
            alpha = jnp.exp(m_old - m_new)
            p = jnp.exp(s - m_new)
            l_scr[...] = alpha * l_scr[...] + jnp.sum(p, axis=-1, keepdims=True)
            acc_scr[...] = alpha * acc_scr[...] + jnp.dot(p.astype(BF16), vb, preferred_element_type=F32)
            m_scr[...] = m_new

        def body(j, carry):
            block(j, False)
            return carry

        lax.fori_loop(0, qi, body, 0)
        block(qi, True)
        outs.append(acc_scr[...] / l_scr[...])
    o_ref[...] = jnp.where(lane < FOX_DH, outs[0], outs[1]).astype(BF16)


def _fox(proj, c, ct, batch, seq, tq):
    nq = seq // tq
    np_ = FOX_HEADS // 2
    return pl.pallas_call(
        functools.partial(_fox_kernel, tq=tq),
        grid=(batch, np_, nq),
        in_specs=[
            pl.BlockSpec((tq, 128), lambda b, p, i: (b * nq + i, OFF_FQ // 128 + p)),
            pl.BlockSpec((seq, 128), lambda b, p, i: (b, OFF_FK // 128 + p)),
            pl.BlockSpec((seq, 128), lambda b, p, i: (b, OFF_FV // 128 + p)),
            pl.BlockSpec((tq, 128), lambda b, p, i: (b * nq + i, p)),
            pl.BlockSpec((1, 1, nq, 8, tq), lambda b, p, i: (b, p, 0, 0, 0)),
        ],
        out_specs=pl.BlockSpec((tq, 128), lambda b, p, i: (b * nq + i, p)),
        out_shape=jax.ShapeDtypeStruct((batch * seq, FOX_W), BF16),
        scratch_shapes=[pltpu.VMEM((tq, 1), F32), pltpu.VMEM((tq, 1), F32), pltpu.VMEM((tq, 128), F32)],
        compiler_params=_params("parallel", "parallel", "arbitrary"),
        name="fox",
    )(proj, proj, proj, c, ct)


def _mix_kernel(x_ref, yr_ref, yf_ref, gr_ref, gf_ref, p_ref, ge_ref, be_ref, bg_ref, wro_ref, wfo_ref,
                wout_ref, g1_ref, b1_ref, wpg_ref, bpg_ref, wple_ref, base_ref, ht_ref, *, alpha):
    h0 = _layer_norm(x_ref[...], ge_ref[...], be_ref[...])
    y_ret = jnp.dot(yr_ref[...], wro_ref[...], preferred_element_type=F32)
    y_fox = jnp.dot(yf_ref[...], wfo_ref[...], preferred_element_type=F32)
    merged = (jax.nn.sigmoid(gr_ref[...].astype(F32) + bg_ref[0:1, :]) * y_ret
              + jax.nn.sigmoid(gf_ref[...].astype(F32) + bg_ref[1:2, :]) * y_fox)
    out = jnp.dot(merged.astype(BF16), wout_ref[...], preferred_element_type=F32)
    h1 = _layer_norm(alpha * h0 + out, g1_ref[...], b1_ref[...])
    h1b = h1.astype(BF16)
    gate = jax.nn.sigmoid(jnp.dot(h1b, wpg_ref[...], preferred_element_type=F32) + bpg_ref[...])
    ple = gate * jnp.dot(p_ref[...].astype(BF16), wple_ref[...], preferred_element_type=F32)
    base_ref[...] = alpha * h1 + ple
    ht_ref[...] = h1.T.astype(BF16)


def _mix(x2, yr, yf, proj, p2, ge, be, bg, wro, wfo, wout, g1, b1, wpg, bpg, wple, alpha, tm=512):
    t = x2.shape[0]
    full = lambda shape: pl.BlockSpec(shape, lambda i: (0,) * len(shape))
    return pl.pallas_call(
        functools.partial(_mix_kernel, alpha=alpha),
        grid=(t // tm,),
        in_specs=[
            pl.BlockSpec((tm, D_MODEL), lambda i: (i, 0)),
            pl.BlockSpec((tm, RET_V_W), lambda i: (i, 0)),
            pl.BlockSpec((tm, FOX_W), lambda i: (i, 0)),
            pl.BlockSpec((tm, D_MODEL), lambda i: (i, OFF_GR // D_MODEL)),
            pl.BlockSpec((tm, D_MODEL), lambda i: (i, OFF_GF // D_MODEL)),
            pl.BlockSpec((tm, PLE_DIM), lambda i: (i, 0)),
            full((1, D_MODEL)), full((1, D_MODEL)), full((2, D_MODEL)),
            full((RET_V_W, D_MODEL)), full((FOX_W, D_MODEL)), full((D_MODEL, D_MODEL)),
            full((1, D_MODEL)), full((1, D_MODEL)),
            full((D_MODEL, D_MODEL)), full((1, D_MODEL)), full((PLE_DIM, D_MODEL)),
        ],
        out_specs=[
            pl.BlockSpec((tm, D_MODEL), lambda i: (i, 0)),
            pl.BlockSpec((D_MODEL, tm), lambda i: (0, i)),
        ],
        out_shape=[jax.ShapeDtypeStruct((t, D_MODEL), F32), jax.ShapeDtypeStruct((D_MODEL, t), BF16)],
        compiler_params=_params("parallel"),
        name="mix",
    )(x2, yr, yf, proj, proj, p2, ge, be, bg, wro, wfo, wout, g1, b1, wpg, bpg, wple)


def _top16(sc):
    nk = sc.shape[0]
    iota = lax.broadcasted_iota(jnp.int32, sc.shape, 0).astype(F32)
    rank = jnp.full(sc.shape, float(PEER_TOPK), F32)
    vals = []
    for a in range(PEER_TOPK):
        m = jnp.max(sc, axis=0, keepdims=True)
        first = jnp.min(jnp.where(sc == m, iota, float(nk)), axis=0, keepdims=True)
        hit = iota == first
        sc = jnp.where(hit, NEG_BIG, sc)
        rank = jnp.where(hit, float(a), rank)
        vals.append(m)
    return vals, rank


def _route_kernel(ht_ref, wq_ref, keys_ref, ct_ref, w0_ref, r1_ref, w1_ref,
                  q_scr, sc_scr, rank_scr, val_scr, cnt_scr, z_scr, *, lanes):
    tb = ht_ref.shape[1]
    q_scr[...] = jnp.dot(wq_ref[...], ht_ref[...], preferred_element_type=F32).astype(BF16)
    val_scr[...] = jnp.zeros_like(val_scr)

    def per_vec(hc, carry):
        row = pl.multiple_of(hc * PEER_HALF, PEER_HALF)
        sc = jnp.dot(keys_ref[hc], q_scr[pl.ds(row, PEER_HALF), :], preferred_element_type=F32)
        sc_scr[hc] = sc
        h = hc // 2
        c = hc % 2
        own_row = lax.broadcasted_iota(jnp.int32, (PEER_HEADS, lanes), 0) == h
        for g in range(tb // lanes):
            sl = slice(g * lanes, (g + 1) * lanes)
            vals, rank = _top16(sc[:, sl])
            rank_scr[hc, :, sl] = rank
            for a in range(PEER_TOPK):
                val_scr[c, a, :, sl] = jnp.where(own_row, vals[a], val_scr[c, a, :, sl])
        return carry

    lax.fori_loop(0, 2 * PEER_HEADS, per_vec, 0)

    v0 = [val_scr[0, a] for a in range(PEER_TOPK)]
    v1 = [val_scr[1, b] for b in range(PEER_TOPK)]
    cand = [v0[a] + v1[b] for a, b in CAND_CELLS]
    flat = [float(a * PEER_TOPK + b) for a, b in CAND_CELLS]
    sel = [jnp.zeros_like(v0[0]) for _ in CAND_CELLS]
    for _ in range(PEER_TOPK):
        m = functools.reduce(jnp.maximum, cand)
        first = functools.reduce(jnp.minimum,
                                 [jnp.where(cv == m, fi, 1e9) for cv, fi in zip(cand, flat)])
        for t, fi in enumerate(flat):
            hit = first == fi
            cand[t] = jnp.where(hit, NEG_BIG, cand[t])
            sel[t] = jnp.where(hit, 1.0, sel[t])
    e0 = [jnp.exp(v0[a] - v0[0]) for a in range(PEER_TOPK)]
    e1 = [jnp.exp(v1[b] - v1[0]) for b in range(PEER_TOPK)]
    z = jnp.zeros_like(v0[0])
    cnt = [jnp.zeros_like(v0[0]) for _ in range(PEER_TOPK)]
    for t, (a, b) in enumerate(CAND_CELLS):
        z = z + sel[t] * (e0[a] * e1[b])
        cnt[a] = cnt[a] + sel[t]
    z_scr[...] = 1.0 / z
    for a in range(PEER_TOPK):
        cnt_scr[a] = cnt[a]

    for h in range(PEER_HEADS):
        s0 = sc_scr[2 * h]
        s1 = sc_scr[2 * h + 1]
        rank0 = rank_scr[2 * h]
        ct = jnp.zeros_like(s0)
        for a in range(PEER_TOPK):
            ct = jnp.where(rank0 == float(a), cnt_scr[a, h:h + 1, :], ct)
        ct_ref[h] = ct
        w0_ref[h] = jnp.exp(s0 - val_scr[0, 0, h:h + 1, :]) * z_scr[h:h + 1, :]
        w1_ref[h] = jnp.exp(s1 - val_scr[1, 0, h:h + 1, :]).astype(BF16)
        r1_ref[h] = rank_scr[2 * h + 1].astype(BF16)


def _route(ht, wq_t, keys, tb=256, lanes=128):
    t = ht.shape[1]
    nq = 2 * PEER_HEADS * PEER_HALF
    routed = lambda dt: jax.ShapeDtypeStruct((PEER_HEADS, PEER_NKEYS, t), dt)
    rspec = pl.BlockSpec((PEER_HEADS, PEER_NKEYS, tb), lambda n: (0, 0, n))
    return pl.pallas_call(
        functools.partial(_route_kernel, lanes=lanes),
        grid=(t // tb,),
        in_specs=[
            pl.BlockSpec((D_MODEL, tb), lambda n: (0, n)),
            pl.BlockSpec((nq, D_MODEL), lambda n: (0, 0)),
            pl.BlockSpec((2 * PEER_HEADS, PEER_NKEYS, PEER_HALF), lambda n: (0, 0, 0)),
        ],
        out_specs=[rspec, rspec, rspec, rspec],
        out_shape=[routed(F32), routed(F32), routed(BF16), routed(BF16)],
        scratch_shapes=[
            pltpu.VMEM((nq, tb), BF16),
            pltpu.VMEM((2 * PEER_HEADS, PEER_NKEYS, tb), F32),
            pltpu.VMEM((2 * PEER_HEADS, PEER_NKEYS, tb), F32),
            pltpu.VMEM((2, PEER_TOPK, PEER_HEADS, tb), F32),
            pltpu.VMEM((PEER_TOPK, PEER_HEADS, tb), F32),
            pltpu.VMEM((PEER_HEADS, tb), F32),
        ],
        compiler_params=_params("parallel"),
        name="route",
    )(ht, wq_t, keys)


def _peer_kernel(ht_ref, u_ref, vt_ref, ct_ref, w0_ref, r1_ref, w1_ref, base_ref, g2_ref, b2_ref,
                 o_ref, hid_scr, a_scr, acc_scr, *, eb, lanes):
    e = pl.program_id(1)
    tb = ht_ref.shape[1]
    sub = 16

    @pl.when(e == 0)
    def _():
        acc_scr[...] = jnp.zeros_like(acc_scr)

    hid_scr[...] = jnp.dot(u_ref[...], ht_ref[...], preferred_element_type=F32)

    def per_key(il, carry):
        base_row = il * PEER_NKEYS
        for g in range(tb // lanes):
            sl = slice(g * lanes, (g + 1) * lanes)
            cts = [jnp.broadcast_to(ct_ref[h, il, :, sl], (sub, lanes)).astype(BF16)
                   for h in range(PEER_HEADS)]
            w0s = [jnp.broadcast_to(w0_ref[h, il, :, sl], (sub, lanes)).astype(BF16)
                   for h in range(PEER_HEADS)]
            for jb in range(PEER_NKEYS // sub):
                js = slice(jb * sub, (jb + 1) * sub)
                gsum = jnp.zeros((sub, lanes), BF16)
                for h in range(PEER_HEADS):
                    w = w1_ref[h, js, sl] * w0s[h]
                    gsum = gsum + jnp.where(r1_ref[h, js, sl] < cts[h], w, jnp.zeros_like(w))
                row = pl.multiple_of(base_row + jb * sub, sub)
                hv = hid_scr[pl.ds(row, sub), sl]
                act = 0.5 * hv * (1.0 + lax.erf(hv * (1.0 / math.sqrt(2.0))))
                a_scr[pl.ds(row, sub), sl] = act.astype(BF16) * gsum
        return carry

    lax.fori_loop(0, eb // PEER_NKEYS, per_key, 0)
    acc_scr[...] += jnp.dot(vt_ref[...], a_scr[...], preferred_element_type=F32)

    @pl.when(e == pl.num_programs(1) - 1)
    def _():
        o_ref[...] = _layer_norm(base_ref[...] + acc_scr[...].T, g2_ref[...], b2_ref[...])


def _peer(ht, u_bf, vt_bf, ct, w0, r1, w1, base, g2, b2, tb=512, eb=512, lanes=128):
    t = ht.shape[1]
    rspec = pl.BlockSpec((PEER_HEADS, PEER_NKEYS, tb), lambda n, e: (0, 0, n))
    kspec = pl.BlockSpec((PEER_HEADS, eb // PEER_NKEYS, 1, tb), lambda n, e: (0, e, 0, n))
    ct = ct.reshape(PEER_HEADS, PEER_NKEYS, 1, t)
    w0 = w0.reshape(PEER_HEADS, PEER_NKEYS, 1, t)
    return pl.pallas_call(
        functools.partial(_peer_kernel, eb=eb, lanes=lanes),
        grid=(t // tb, PEER_NEXP // eb),
        in_specs=[
            pl.BlockSpec((D_MODEL, tb), lambda n, e: (0, n)),
            pl.BlockSpec((eb, D_MODEL), lambda n, e: (e, 0)),
            pl.BlockSpec((D_MODEL, eb), lambda n, e: (0, e)),
            kspec, kspec, rspec, rspec,
            pl.BlockSpec((tb, D_MODEL), lambda n, e: (n, 0)),
            pl.BlockSpec((1, D_MODEL), lambda n, e: (0, 0)),
            pl.BlockSpec((1, D_MODEL), lambda n, e: (0, 0)),
        ],
        out_specs=pl.BlockSpec((tb, D_MODEL), lambda n, e: (n, 0)),
        out_shape=jax.ShapeDtypeStruct((t, D_MODEL), F32),
        scratch_shapes=[
            pltpu.VMEM((eb, tb), F32),
            pltpu.VMEM((eb, tb), BF16),
            pltpu.VMEM((D_MODEL, tb), F32),
        ],
        compiler_params=_params("parallel", "arbitrary"),
        name="peer",
    )(ht, u_bf, vt_bf, ct, w0, r1, w1, base, g2, b2)


def _retention_tables(seq):
    half = RET_DK // 2
    inv = ROPE_BASE ** (-jnp.arange(half, dtype=F32) / half)
    ang = jnp.arange(seq, dtype=jnp.int32).astype(F32)[:, None] * inv[None, :]
    cos, sin = jnp.cos(ang), jnp.sin(ang)
    cos_t = jnp.concatenate([cos, cos], axis=-1)
    sin_t = jnp.concatenate([-sin, sin], axis=-1)
    c = RET_CHUNK
    log_g = jnp.log(1.0 - 2.0 ** (-5.0 - jnp.arange(RET_HEADS, dtype=F32)))
    idx = jnp.arange(c, dtype=F32)
    diff = idx[:, None] - idx[None, :]
    intra = jnp.where(diff >= 0, jnp.exp(log_g[:, None, None] * jnp.maximum(diff, 0.0)), 0.0)
    qdec = jnp.broadcast_to(jnp.exp(log_g[:, None] * (idx + 1.0))[..., None], (RET_HEADS, c, RET_DK))
    kdec = jnp.broadcast_to(jnp.exp(log_g[:, None] * (c - 1.0 - idx))[..., None], (RET_HEADS, c, RET_DK))
    chunk_decay = tuple(math.exp(math.log(1.0 - 2.0 ** (-5.0 - h)) * c) for h in range(RET_HEADS))
    return cos_t, sin_t, intra, qdec, kdec, chunk_decay


def _pack_w_in(w):
    rq, rk, rv, rg, fq, fk, fv, ff, gr, gf = jnp.split(
        w, [512, 1024, 2048, 3072, 3584, 4096, 4608, 4616, 5640], axis=-1)
    main = jnp.concatenate([rq, rk, rv, rg, gr, gf, fq, fk, fv], axis=-1).astype(BF16)
    ff_pad = jnp.zeros((w.shape[0], FOX_HEADS // 2, 128), w.dtype).at[:, :, 0:2].set(
        ff.reshape(w.shape[0], FOX_HEADS // 2, 2)).reshape(w.shape[0], FF_W).astype(BF16)
    return main, ff_pad


def kernel(x, p, ln_emb_g, ln_emb_b, w_in, b_forget, b_branch_gate, w_ret_o, w_fox_o, w_out, ln1_g, ln1_b,
           w_peer_q, peer_sub_keys, peer_u, peer_v, w_ple_gate, b_ple_gate, w_ple, ln2_g, ln2_b):
    batch, seq, d = x.shape
    depth = w_in.shape[0]
    assert depth == 1 and d == D_MODEL, "the trunk-entry norm is fused into the single layer"
    t = batch * seq
    alpha = (2.0 * depth) ** 0.25
    fox_tq = min(256, seq)
    row = lambda v: v.reshape(1, -1).astype(F32)
    cos_t, sin_t, intra, qdec, kdec, chunk_decay = _retention_tables(seq)

    x2 = x.reshape(t, d)
    ge, be = row(ln_emb_g), row(ln_emb_b)
    w_main, w_ff = _pack_w_in(w_in[0])
    bf_pad = jnp.zeros((FOX_HEADS // 2, 128), F32).at[:, 0:2].set(
        b_forget[0].astype(F32).reshape(FOX_HEADS // 2, 2)).reshape(1, FF_W)
    proj, ff = _in_proj(x2, ge, be, w_main, w_ff)
    yr = _retention(proj, cos_t, sin_t, intra, qdec, kdec, chunk_decay, batch, seq)
    c, ct = _fcumsum(ff, bf_pad, batch, seq, fox_tq)
    yf = _fox(proj, c, ct, batch, seq, fox_tq)
    base, ht = _mix(x2, yr, yf, proj, p[0].reshape(t, PLE_DIM), ge, be, b_branch_gate[0].astype(F32),
                    w_ret_o[0].astype(BF16), w_fox_o[0].astype(BF16), w_out[0].astype(BF16),
                    row(ln1_g[0]), row(ln1_b[0]), w_ple_gate[0].astype(BF16), row(b_ple_gate[0]),
                    w_ple[0].astype(BF16), alpha)
    keys = peer_sub_keys[0].reshape(2 * PEER_HEADS, PEER_NKEYS, PEER_HALF).astype(BF16)
    ct_r, w0, r1, w1 = _route(ht, w_peer_q[0].T.astype(BF16), keys)
    out = _peer(ht, peer_u[0].astype(BF16), peer_v[0].T.astype(BF16), ct_r, w0, r1, w1, base,
                row(ln2_g[0]), row(ln2_b[0]))
    return out.reshape(batch, seq, d)
```

```python
import functools
import math

import jax
import jax.numpy as jnp
from jax import lax
from jax.experimental import pallas as pl
from jax.experimental.pallas import tpu as pltpu

F32 = jnp.float32
BF16 = jnp.bfloat16

D_MODEL = 1024
RET_HEADS = 4
RET_DK = 128
RET_DV = 256
RET_CHUNK = 128
FOX_HEADS = 8
FOX_DH = 64
PEER_HEADS = 8
PEER_NKEYS = 128
PEER_NEXP = PEER_NKEYS * PEER_NKEYS
PEER_HALF = 128
PEER_TOPK = 16
PLE_DIM = 256
LN_EPS = 1e-5
ROPE_BASE = 10000.0

RET_QK_W = RET_HEADS * RET_DK
RET_V_W = RET_HEADS * RET_DV
FOX_W = FOX_HEADS * FOX_DH

OFF_RQ, OFF_RK, OFF_RV, OFF_RG = 0, 512, 1024, 2048
OFF_GR, OFF_GF = 3072, 4096
OFF_FQ, OFF_FK, OFF_FV = 5120, 5632, 6144
PROJ_W = 6656
FF_W = 512

V7X_VMEM_LIMIT = 56 * 1024 * 1024
NEG_BIG = -3.0e38
CAND_CELLS = tuple((a, b) for a in range(PEER_TOPK) for b in range(PEER_TOPK)
                   if (a + 1) * (b + 1) <= PEER_TOPK)


def _layer_norm(x, g, b):
    mu = jnp.mean(x, axis=-1, keepdims=True)
    xc = x - mu
    var = jnp.mean(xc * xc, axis=-1, keepdims=True)
    return xc * lax.rsqrt(var + LN_EPS) * g + b


def _params(*sem):
    return pltpu.CompilerParams(dimension_semantics=sem, vmem_limit_bytes=V7X_VMEM_LIMIT)


def _inproj_kernel(x_ref, g_ref, b_ref, w_ref, wff_ref, o_ref, ff_ref, h_scr):
    @pl.when(pl.program_id(1) == 0)
    def _():
        hb = _layer_norm(x_ref[...], g_ref[...], b_ref[...]).astype(BF16)
        h_scr[...] = hb
        ff_ref[...] = jnp.dot(hb, wff_ref[...], preferred_element_type=F32)

    o_ref[...] = jnp.dot(h_scr[...], w_ref[...], preferred_element_type=F32).astype(BF16)


def _in_proj(x2, g, b, w_main, w_ff, tm=512, tn=1664):
    t = x2.shape[0]
    return pl.pallas_call(
        _inproj_kernel,
        grid=(t // tm, PROJ_W // tn),
        in_specs=[
            pl.BlockSpec((tm, D_MODEL), lambda i, j: (i, 0)),
            pl.BlockSpec((1, D_MODEL), lambda i, j: (0, 0)),
            pl.BlockSpec((1, D_MODEL), lambda i, j: (0, 0)),
            pl.BlockSpec((D_MODEL, tn), lambda i, j: (0, j)),
            pl.BlockSpec((D_MODEL, FF_W), lambda i, j: (0, 0)),
        ],
        out_specs=[
            pl.BlockSpec((tm, tn), lambda i, j: (i, j)),
            pl.BlockSpec((tm, FF_W), lambda i, j: (i, 0)),
        ],
        out_shape=[jax.ShapeDtypeStruct((t, PROJ_W), BF16), jax.ShapeDtypeStruct((t, FF_W), F32)],
        scratch_shapes=[pltpu.VMEM((tm, D_MODEL), BF16)],
        compiler_params=_params("parallel", "arbitrary"),
        name="in_proj",
    )(x2, g, b, w_main, w_ff)


def _retention_kernel(q_ref, k_ref, v_ref, g_ref, cos_ref, sin_ref, intra_ref, qd_ref, kd_ref,
                      o_ref, r_scr, *, chunk_decay):
    @pl.when(pl.program_id(1) == 0)
    def _():
        r_scr[...] = jnp.zeros_like(r_scr)

    cos = cos_ref[...]
    sin = sin_ref[...]
    for h in range(RET_HEADS):
        q = q_ref[:, h * RET_DK:(h + 1) * RET_DK].astype(F32)
        k = k_ref[:, h * RET_DK:(h + 1) * RET_DK].astype(F32)
        v = v_ref[:, h * RET_DV:(h + 1) * RET_DV]
        qr = q * cos + pltpu.roll(q, RET_DK // 2, 1) * sin
        kr = (k * cos + pltpu.roll(k, RET_DK // 2, 1) * sin) * (RET_DK ** -0.5)
        s = lax.dot_general(qr.astype(BF16), kr.astype(BF16), (((1,), (1,)), ((), ())),
                            preferred_element_type=F32) * intra_ref[h]
        inner = jnp.dot(s.astype(BF16), v, preferred_element_type=F32)
        r_old = r_scr[h]
        cross = jnp.dot((qr * qd_ref[h]).astype(BF16), r_old.astype(BF16), preferred_element_type=F32)
        kv = lax.dot_general((kr * kd_ref[h]).astype(BF16), v, (((0,), (0,)), ((), ())),
                             preferred_element_type=F32)
        r_scr[h] = chunk_decay[h] * r_old + kv
        y = inner + cross
        mu = jnp.mean(y, axis=-1, keepdims=True)
        yc = y - mu
        var = jnp.mean(yc * yc, axis=-1, keepdims=True)
        yn = yc * lax.rsqrt(var + LN_EPS)
        gate = g_ref[:, h * RET_DV:(h + 1) * RET_DV].astype(F32)
        o_ref[:, h * RET_DV:(h + 1) * RET_DV] = (gate * jax.nn.sigmoid(gate) * yn).astype(BF16)


def _retention(proj, cos_t, sin_t, intra, qdec, kdec, chunk_decay, batch, seq):
    c = RET_CHUNK
    nc = seq // c
    row = lambda b, j: b * nc + j
    return pl.pallas_call(
        functools.partial(_retention_kernel, chunk_decay=chunk_decay),
        grid=(batch, nc),
        in_specs=[
            pl.BlockSpec((c, RET_QK_W), lambda b, j: (row(b, j), OFF_RQ // RET_QK_W)),
            pl.BlockSpec((c, RET_QK_W), lambda b, j: (row(b, j), OFF_RK // RET_QK_W)),
            pl.BlockSpec((c, RET_V_W), lambda b, j: (row(b, j), OFF_RV // RET_V_W)),
            pl.BlockSpec((c, RET_V_W), lambda b, j: (row(b, j), OFF_RG // RET_V_W)),
            pl.BlockSpec((c, RET_DK), lambda b, j: (j, 0)),
            pl.BlockSpec((c, RET_DK), lambda b, j: (j, 0)),
            pl.BlockSpec((RET_HEADS, c, c), lambda b, j: (0, 0, 0)),
            pl.BlockSpec((RET_HEADS, c, RET_DK), lambda b, j: (0, 0, 0)),
            pl.BlockSpec((RET_HEADS, c, RET_DK), lambda b, j: (0, 0, 0)),
        ],
        out_specs=pl.BlockSpec((c, RET_V_W), lambda b, j: (row(b, j), 0)),
        out_shape=jax.ShapeDtypeStruct((batch * seq, RET_V_W), BF16),
        scratch_shapes=[pltpu.VMEM((RET_HEADS, RET_DK, RET_DV), F32)],
        compiler_params=_params("parallel", "arbitrary"),
        name="retention",
    )(proj, proj, proj, proj, cos_t, sin_t, intra, qdec, kdec)


def _fcumsum_kernel(ff_ref, bf_ref, tri_ref, c_ref, ct_ref, *, blk):
    seq = ff_ref.shape[0]
    carry = jnp.zeros((1, FF_W), F32)
    for i in range(seq // blk):
        lf = jax.nn.log_sigmoid(ff_ref[i * blk:(i + 1) * blk, :] + bf_ref[...])
        cs = jnp.dot(tri_ref[...], lf, preferred_element_type=F32,
                     precision=lax.Precision.HIGHEST) + carry
        c_ref[i * blk:(i + 1) * blk, :] = cs
        carry = cs[blk - 1:blk, :]
        for p in range(FOX_HEADS // 2):
            ct_ref[0, p, i] = cs[:, p * 128:(p + 1) * 128].T[0:8, :]


def _fcumsum(ff, bf_pad, batch, seq, blk):
    tri = (lax.broadcasted_iota(jnp.int32, (blk, blk), 0)
           >= lax.broadcasted_iota(jnp.int32, (blk, blk), 1)).astype(F32)
    nb = seq // blk
    return pl.pallas_call(
        functools.partial(_fcumsum_kernel, blk=blk),
        grid=(batch,),
        in_specs=[
            pl.BlockSpec((seq, FF_W), lambda b: (b, 0)),
            pl.BlockSpec((1, FF_W), lambda b: (0, 0)),
            pl.BlockSpec((blk, blk), lambda b: (0, 0)),
        ],
        out_specs=[
            pl.BlockSpec((seq, FF_W), lambda b: (b, 0)),
            pl.BlockSpec((1, FOX_HEADS // 2, nb, 8, blk), lambda b: (b, 0, 0, 0, 0)),
        ],
        out_shape=[jax.ShapeDtypeStruct((batch * seq, FF_W), F32),
                   jax.ShapeDtypeStruct((batch, FOX_HEADS // 2, nb, 8, blk), F32)],
        compiler_params=_params("parallel"),
        name="fcumsum",
    )(ff, bf_pad, tri)


def _fox_kernel(q_ref, k_ref, v_ref, c_ref, ct_ref, o_ref, vt_scr, ckb_scr, m_scr, l_scr, acc_scr, *, tq):
    qi = pl.program_id(2)
    seq = k_ref.shape[0]
    nkv = seq // tq

    @pl.when(qi == 0)
    def _():
        for j in range(nkv):
            vt_scr[j] = v_ref[j * tq:(j + 1) * tq, :].astype(F32).T.astype(BF16)
        for r in range(2):
            ckb_scr[r] = jnp.broadcast_to(c_ref[:, r:r + 1], (seq, 128))

    lane = lax.broadcasted_iota(jnp.int32, (tq, 128), 1)
    key_i = lax.broadcasted_iota(jnp.int32, (tq, tq), 0)
    qry_i = lax.broadcasted_iota(jnp.int32, (tq, tq), 1)
    q = q_ref[...] * jnp.asarray(FOX_DH ** -0.5, BF16)
    qh = [jnp.where((lane >= r * FOX_DH) & (lane < (r + 1) * FOX_DH), q, jnp.zeros_like(q)) for r in range(2)]
    cq = [ct_ref[0, 0, qi, r:r + 1, :] for r in range(2)]
    m_scr[...] = jnp.full_like(m_scr, NEG_BIG)
    l_scr[...] = jnp.zeros_like(l_scr)
    acc_scr[...] = jnp.zeros_like(acc_scr)

    def block(j, diag):
        start = pl.multiple_of(j * tq, tq)
        kb = k_ref[pl.ds(start, tq), :]
        vtb = vt_scr[j]
        s = [lax.dot_general(kb, qh[r], (((1,), (1,)), ((), ())), preferred_element_type=F32)
             for r in range(2)]
        s = [s[r] + cq[r] - pltpu.repeat(ckb_scr[r, pl.ds(start, tq), :], tq // 128, axis=1)
             for r in range(2)]
        if diag:
            s = [jnp.where(key_i <= qry_i, s[r], -1e30) for r in range(2)]
        m_old = [m_scr[r] for r in range(2)]
        m_new = [jnp.maximum(m_old[r], jnp.max(s[r], axis=0, keepdims=True)) for r in range(2)]
        alpha = [jnp.exp(m_old[r] - m_new[r]) for r in range(2)]
        p = [jnp.exp(s[r] - m_new[r]) for r in range(2)]
        for r in range(2):
            l_scr[r] = alpha[r] * l_scr[r] + jnp.sum(p[r], axis=0, keepdims=True)
            pv = jnp.dot(vtb[r * FOX_DH:(r + 1) * FOX_DH, :], p[r].astype(BF16), preferred_element_type=F32)
            acc_scr[r] = alpha[r] * acc_scr[r] + pv
            m_scr[r] = m_new[r]

    def body(j, carry):
        block(j, False)
        return carry

    lax.fori_loop(0, qi, body, 0)
    block(qi, True)
    out_t = jnp.concatenate([acc_scr[r] / l_scr[r] for r in range(2)], axis=0)
    o_ref[...] = out_t.T.astype(BF16)


def _fox(proj, c, ct, batch, seq, tq):
    nq = seq // tq
    np_ = FOX_HEADS // 2
    return pl.pallas_call(
        functools.partial(_fox_kernel, tq=tq),
        grid=(batch, np_, nq),
        in_specs=[
            pl.BlockSpec((tq, 128), lambda b, p, i: (b * nq + i, OFF_FQ // 128 + p)),
            pl.BlockSpec((seq, 128), lambda b, p, i: (b, OFF_FK // 128 + p)),
            pl.BlockSpec((seq, 128), lambda b, p, i: (b, OFF_FV // 128 + p)),
            pl.BlockSpec((seq, 128), lambda b, p, i: (b, p)),
            pl.BlockSpec((1, 1, nq, 8, tq), lambda b, p, i: (b, p, 0, 0, 0)),
        ],
        out_specs=pl.BlockSpec((tq, 128), lambda b, p, i: (b * nq + i, p)),
        out_shape=jax.ShapeDtypeStruct((batch * seq, FOX_W), BF16),
        scratch_shapes=[
            pltpu.VMEM((nq, 128, tq), BF16),
            pltpu.VMEM((2, seq, 128), F32),
            pltpu.VMEM((2, 1, tq), F32),
            pltpu.VMEM((2, 1, tq), F32),
            pltpu.VMEM((2, FOX_DH, tq), F32),
        ],
        compiler_params=_params("parallel", "parallel", "arbitrary"),
        name="fox",
    )(proj, proj, proj, c, ct)


def _mix_kernel(x_ref, yr_ref, yf_ref, gr_ref, gf_ref, p_ref, ge_ref, be_ref, bg_ref, wro_ref, wfo_ref,
                wout_ref, g1_ref, b1_ref, wpg_ref, bpg_ref, wple_ref, base_ref, ht_ref, *, alpha):
    h0 = _layer_norm(x_ref[...], ge_ref[...], be_ref[...])
    y_ret = jnp.dot(yr_ref[...], wro_ref[...], preferred_element_type=F32)
    y_fox = jnp.dot(yf_ref[...], wfo_ref[...], preferred_element_type=F32)
    merged = (jax.nn.sigmoid(gr_ref[...].astype(F32) + bg_ref[0:1, :]) * y_ret
              + jax.nn.sigmoid(gf_ref[...].astype(F32) + bg_ref[1:2, :]) * y_fox)
    out = jnp.dot(merged.astype(BF16), wout_ref[...], preferred_element_type=F32)
    h1 = _layer_norm(alpha * h0 + out, g1_ref[...], b1_ref[...])
    h1b = h1.astype(BF16)
    gate = jax.nn.sigmoid(jnp.dot(h1b, wpg_ref[...], preferred_element_type=F32) + bpg_ref[...])
    ple = gate * jnp.dot(p_ref[...].astype(BF16), wple_ref[...], preferred_element_type=F32)
    base_ref[...] = alpha * h1 + ple
    ht_ref[...] = h1.T.astype(BF16)


def _mix(x2, yr, yf, proj, p2, ge, be, bg, wro, wfo, wout, g1, b1, wpg, bpg, wple, alpha, tm=512):
    t = x2.shape[0]
    full = lambda shape: pl.BlockSpec(shape, lambda i: (0,) * len(shape))
    return pl.pallas_call(
        functools.partial(_mix_kernel, alpha=alpha),
        grid=(t // tm,),
        in_specs=[
            pl.BlockSpec((tm, D_MODEL), lambda i: (i, 0)),
            pl.BlockSpec((tm, RET_V_W), lambda i: (i, 0)),
            pl.BlockSpec((tm, FOX_W), lambda i: (i, 0)),
            pl.BlockSpec((tm, D_MODEL), lambda i: (i, OFF_GR // D_MODEL)),
            pl.BlockSpec((tm, D_MODEL), lambda i: (i, OFF_GF // D_MODEL)),
            pl.BlockSpec((tm, PLE_DIM), lambda i: (i, 0)),
            full((1, D_MODEL)), full((1, D_MODEL)), full((2, D_MODEL)),
            full((RET_V_W, D_MODEL)), full((FOX_W, D_MODEL)), full((D_MODEL, D_MODEL)),
            full((1, D_MODEL)), full((1, D_MODEL)),
            full((D_MODEL, D_MODEL)), full((1, D_MODEL)), full((PLE_DIM, D_MODEL)),
        ],
        out_specs=[
            pl.BlockSpec((tm, D_MODEL), lambda i: (i, 0)),
            pl.BlockSpec((D_MODEL, tm), lambda i: (0, i)),
        ],
        out_shape=[jax.ShapeDtypeStruct((t, D_MODEL), F32), jax.ShapeDtypeStruct((D_MODEL, t), BF16)],
        compiler_params=_params("parallel"),
        name="mix",
    )(x2, yr, yf, proj, proj, p2, ge, be, bg, wro, wfo, wout, g1, b1, wpg, bpg, wple)


def _top16(sc):
    nk = sc.shape[0]
    iota = lax.broadcasted_iota(jnp.int32, sc.shape, 0).astype(F32)
    rank = jnp.full(sc.shape, float(PEER_TOPK), F32)
    vals = []
    for a in range(PEER_TOPK):
        m = jnp.max(sc, axis=0, keepdims=True)
        first = jnp.min(jnp.where(sc == m, iota, float(nk)), axis=0, keepdims=True)
        hit = iota == first
        sc = jnp.where(hit, NEG_BIG, sc)
        rank = jnp.where(hit, float(a), rank)
        vals.append(m)
    return vals, rank


def _route_kernel(ht_ref, wq_ref, keys_ref, ct_ref, w0_ref, r1_ref, w1_ref,
                  q_scr, sc_scr, rank_scr, val_scr, cnt_scr, z_scr, *, lanes):
    tb = ht_ref.shape[1]
    q_scr[...] = jnp.dot(wq_ref[...], ht_ref[...], preferred_element_type=F32).astype(BF16)
    val_scr[...] = jnp.zeros_like(val_scr)

    def per_vec(hc, carry):
        row = pl.multiple_of(hc * PEER_HALF, PEER_HALF)
        sc = jnp.dot(keys_ref[hc], q_scr[pl.ds(row, PEER_HALF), :], preferred_element_type=F32)
        sc_scr[hc] = sc
        h = hc // 2
        c = hc % 2
        own_row = lax.broadcasted_iota(jnp.int32, (PEER_HEADS, lanes), 0) == h
        for g in range(tb // lanes):
            sl = slice(g * lanes, (g + 1) * lanes)
            vals, rank = _top16(sc[:, sl])
            rank_scr[hc, :, sl] = rank
            for a in range(PEER_TOPK):
                val_scr[c, a, :, sl] = jnp.where(own_row, vals[a], val_scr[c, a, :, sl])
        return carry

    lax.fori_loop(0, 2 * PEER_HEADS, per_vec, 0)

    v0 = [val_scr[0, a] for a in range(PEER_TOPK)]
    v1 = [val_scr[1, b] for b in range(PEER_TOPK)]
    cand = [v0[a] + v1[b] for a, b in CAND_CELLS]
    flat = [float(a * PEER_TOPK + b) for a, b in CAND_CELLS]
    sel = [jnp.zeros_like(v0[0]) for _ in CAND_CELLS]
    for _ in range(PEER_TOPK):
        m = functools.reduce(jnp.maximum, cand)
        first = functools.reduce(jnp.minimum,
                                 [jnp.where(cv == m, fi, 1e9) for cv, fi in zip(cand, flat)])
        for t, fi in enumerate(flat):
            hit = first == fi
            cand[t] = jnp.where(hit, NEG_BIG, cand[t])
            sel[t] = jnp.where(hit, 1.0, sel[t])
    e0 = [jnp.exp(v0[a] - v0[0]) for a in range(PEER_TOPK)]
    e1 = [jnp.exp(v1[b] - v1[0]) for b in range(PEER_TOPK)]
    z = jnp.zeros_like(v0[0])
    cnt = [jnp.zeros_like(v0[0]) for _ in range(PEER_TOPK)]
    for t, (a, b) in enumerate(CAND_CELLS):
        z = z + sel[t] * (e0[a] * e1[b])
        cnt[a] = cnt[a] + sel[t]
    z_scr[...] = 1.0 / z
    for a in range(PEER_TOPK):
        cnt_scr[a] = cnt[a]

    for h in range(PEER_HEADS):
        s0 = sc_scr[2 * h]
        s1 = sc_scr[2 * h + 1]
        rank0 = rank_scr[2 * h]
        ct = jnp.zeros_like(s0)
        for a in range(PEER_TOPK):
            ct = jnp.where(rank0 == float(a), cnt_scr[a, h:h + 1, :], ct)
        ct_ref[h] = ct
        w0_ref[h] = jnp.exp(s0 - val_scr[0, 0, h:h + 1, :]) * z_scr[h:h + 1, :]
        w1_ref[h] = jnp.exp(s1 - val_scr[1, 0, h:h + 1, :]).astype(BF16)
        r1_ref[h] = rank_scr[2 * h + 1].astype(BF16)


def _route(ht, wq_t, keys, tb=256, lanes=128):
    t = ht.shape[1]
    nq = 2 * PEER_HEADS * PEER_HALF
    routed = lambda dt: jax.ShapeDtypeStruct((PEER_HEADS, PEER_NKEYS, t), dt)
    rspec = pl.BlockSpec((PEER_HEADS, PEER_NKEYS, tb), lambda n: (0, 0, n))
    return pl.pallas_call(
        functools.partial(_route_kernel, lanes=lanes),
        grid=(t // tb,),
        in_specs=[
            pl.BlockSpec((D_MODEL, tb), lambda n: (0, n)),
            pl.BlockSpec((nq, D_MODEL), lambda n: (0, 0)),
            pl.BlockSpec((2 * PEER_HEADS, PEER_NKEYS, PEER_HALF), lambda n: (0, 0, 0)),
        ],
        out_specs=[rspec, rspec, rspec, rspec],
        out_shape=[routed(F32), routed(F32), routed(BF16), routed(BF16)],
        scratch_shapes=[
            pltpu.VMEM((nq, tb), BF16),
            pltpu.VMEM((2 * PEER_HEADS, PEER_NKEYS, tb), F32),
            pltpu.VMEM((2 * PEER_HEADS, PEER_NKEYS, tb), F32),
            pltpu.VMEM((2, PEER_TOPK, PEER_HEADS, tb), F32),
            pltpu.VMEM((PEER_TOPK, PEER_HEADS, tb), F32),
            pltpu.VMEM((PEER_HEADS, tb), F32),
        ],
        compiler_params=_params("parallel"),
        name="route",
    )(ht, wq_t, keys)


def _peer_kernel(ht_ref, u_ref, vt_ref, ct_ref, w0_ref, r1_ref, w1_ref, base_ref, g2_ref, b2_ref,
                 o_ref, hid_scr, a_scr, acc_scr, r1_scr, w1_scr, *, eb, lanes):
    e = pl.program_id(1)
    tb = ht_ref.shape[1]
    sub = 16

    @pl.when(e == 0)
    def _():
        acc_scr[...] = jnp.zeros_like(acc_scr)
        r1_scr[...] = r1_ref[...]
        w1_scr[...] = w1_ref[...]

    hid_scr[...] = jnp.dot(u_ref[...], ht_ref[...], preferred_element_type=F32)
    for il in range(eb // PEER_NKEYS):
        for lg in range(tb // lanes):
            sl = slice(lg * lanes, (lg + 1) * lanes)
            cts = [jnp.broadcast_to(ct_ref[h, il:il + 1, sl], (sub, lanes)).astype(BF16)
                   for h in range(PEER_HEADS)]
            w0s = [jnp.broadcast_to(w0_ref[h, il:il + 1, sl], (sub, lanes)).astype(BF16)
                   for h in range(PEER_HEADS)]
            for jb in range(PEER_NKEYS // sub):
                js = slice(jb * sub, (jb + 1) * sub)
                gsum = None
                for h in range(PEER_HEADS):
                    w = w1_scr[h, js, sl] * w0s[h]
                    term = jnp.where(r1_scr[h, js, sl] < cts[h], w, jnp.zeros_like(w))
                    gsum = term if gsum is None else gsum + term
                rows = slice(il * PEER_NKEYS + jb * sub, il * PEER_NKEYS + (jb + 1) * sub)
                hv = hid_scr[rows, sl]
                act = 0.5 * hv * (1.0 + lax.erf(hv * (1.0 / math.sqrt(2.0))))
                a_scr[rows, sl] = act.astype(BF16) * gsum
    acc_scr[...] += jnp.dot(vt_ref[...], a_scr[...], preferred_element_type=F32)

    @pl.when(e == pl.num_programs(1) - 1)
    def _():
        o_ref[...] = _layer_norm(base_ref[...] + acc_scr[...].T, g2_ref[...], b2_ref[...])


def _peer(ht, u_bf, vt_bf, ct, w0, r1, w1, base, g2, b2, tb=512, eb=1024, lanes=128):
    t = ht.shape[1]
    kr = eb // PEER_NKEYS
    rspec = pl.BlockSpec((PEER_HEADS, PEER_NKEYS, tb), lambda n, e: (0, 0, n))
    kspec = pl.BlockSpec((PEER_HEADS, kr, tb), lambda n, e: (0, e, n))
    return pl.pallas_call(
        functools.partial(_peer_kernel, eb=eb, lanes=lanes),
        grid=(t // tb, PEER_NEXP // eb),
        in_specs=[
            pl.BlockSpec((D_MODEL, tb), lambda n, e: (0, n)),
            pl.BlockSpec((eb, D_MODEL), lambda n, e: (e, 0)),
            pl.BlockSpec((D_MODEL, eb), lambda n, e: (0, e)),
            kspec, kspec, rspec, rspec,
            pl.BlockSpec((tb, D_MODEL), lambda n, e: (n, 0)),
            pl.BlockSpec((1, D_MODEL), lambda n, e: (0, 0)),
            pl.BlockSpec((1, D_MODEL), lambda n, e: (0, 0)),
        ],
        out_specs=pl.BlockSpec((tb, D_MODEL), lambda n, e: (n, 0)),
        out_shape=jax.ShapeDtypeStruct((t, D_MODEL), F32),
        scratch_shapes=[
            pltpu.VMEM((eb, tb), F32),
            pltpu.VMEM((eb, tb), BF16),
            pltpu.VMEM((D_MODEL, tb), F32),
            pltpu.VMEM((PEER_HEADS, PEER_NKEYS, tb), BF16),
            pltpu.VMEM((PEER_HEADS, PEER_NKEYS, tb), BF16),
        ],
        compiler_params=_params("parallel", "arbitrary"),
        name="peer",
    )(ht, u_bf, vt_bf, ct, w0, r1, w1, base, g2, b2)


def _retention_tables(seq):
    half = RET_DK // 2
    inv = ROPE_BASE ** (-jnp.arange(half, dtype=F32) / half)
    ang = jnp.arange(seq, dtype=jnp.int32).astype(F32)[:, None] * inv[None, :]
    cos, sin = jnp.cos(ang), jnp.sin(ang)
    cos_t = jnp.concatenate([cos, cos], axis=-1)
    sin_t = jnp.concatenate([-sin, sin], axis=-1)
    c = RET_CHUNK
    log_g = jnp.log(1.0 - 2.0 ** (-5.0 - jnp.arange(RET_HEADS, dtype=F32)))
    idx = jnp.arange(c, dtype=F32)
    diff = idx[:, None] - idx[None, :]
    intra = jnp.where(diff >= 0, jnp.exp(log_g[:, None, None] * jnp.maximum(diff, 0.0)), 0.0)
    qdec = jnp.broadcast_to(jnp.exp(log_g[:, None] * (idx + 1.0))[..., None], (RET_HEADS, c, RET_DK))
    kdec = jnp.broadcast_to(jnp.exp(log_g[:, None] * (c - 1.0 - idx))[..., None], (RET_HEADS, c, RET_DK))
    chunk_decay = tuple(math.exp(math.log(1.0 - 2.0 ** (-5.0 - h)) * c) for h in range(RET_HEADS))
    return cos_t, sin_t, intra, qdec, kdec, chunk_decay


def _pack_w_in(w):
    rq, rk, rv, rg, fq, fk, fv, ff, gr, gf = jnp.split(
        w, [512, 1024, 2048, 3072, 3584, 4096, 4608, 4616, 5640], axis=-1)
    main = jnp.concatenate([rq, rk, rv, rg, gr, gf, fq, fk, fv], axis=-1).astype(BF16)
    ff_pad = jnp.zeros((w.shape[0], FOX_HEADS // 2, 128), w.dtype).at[:, :, 0:2].set(
        ff.reshape(w.shape[0], FOX_HEADS // 2, 2)).reshape(w.shape[0], FF_W).astype(BF16)
    return main, ff_pad


def kernel(x, p, ln_emb_g, ln_emb_b, w_in, b_forget, b_branch_gate, w_ret_o, w_fox_o, w_out, ln1_g, ln1_b,
           w_peer_q, peer_sub_keys, peer_u, peer_v, w_ple_gate, b_ple_gate, w_ple, ln2_g, ln2_b):
    batch, seq, d = x.shape
    depth = w_in.shape[0]
    assert depth == 1 and d == D_MODEL, "the trunk-entry norm is fused into the single layer"
    t = batch * seq
    alpha = (2.0 * depth) ** 0.25
    fox_tq = min(512, seq)
    row = lambda v: v.reshape(1, -1).astype(F32)
    cos_t, sin_t, intra, qdec, kdec, chunk_decay = _retention_tables(seq)

    x2 = x.reshape(t, d)
    ge, be = row(ln_emb_g), row(ln_emb_b)
    w_main, w_ff = _pack_w_in(w_in[0])
    bf_pad = jnp.zeros((FOX_HEADS // 2, 128), F32).at[:, 0:2].set(
        b_forget[0].astype(F32).reshape(FOX_HEADS // 2, 2)).reshape(1, FF_W)
    proj, ff = _in_proj(x2, ge, be, w_main, w_ff)
    yr = _retention(proj, cos_t, sin_t, intra, qdec, kdec, chunk_decay, batch, seq)
    c, ct = _fcumsum(ff, bf_pad, batch, seq, fox_tq)
    yf = _fox(proj, c, ct, batch, seq, fox_tq)
    base, ht = _mix(x2, yr, yf, proj, p[0].reshape(t, PLE_DIM), ge, be, b_branch_gate[0].astype(F32),
                    w_ret_o[0].astype(BF16), w_fox_o[0].astype(BF16), w_out[0].astype(BF16),
                    row(ln1_g[0]), row(ln1_b[0]), w_ple_gate[0].astype(BF16), row(b_ple_gate[0]),
                    w_ple[0].astype(BF16), alpha)
    keys = peer_sub_keys[0].reshape(2 * PEER_HEADS, PEER_NKEYS, PEER_HALF).astype(BF16)
    ct_r, w0, r1, w1 = _route(ht, w_peer_q[0].T.astype(BF16), keys)
    out = _peer(ht, peer_u[0].astype(BF16), peer_v[0].T.astype(BF16), ct_r, w0, r1, w1, base,
                row(ln2_g[0]), row(ln2_b[0]))
    return out.reshape(batch, seq, d)
```

```python
import functools
import math

import jax
import jax.numpy as jnp
from jax import lax
from jax.experimental import pallas as pl
from jax.experimental.pallas import tpu as pltpu

F32 = jnp.float32
BF16 = jnp.bfloat16

D_MODEL = 1024
RET_HEADS = 4
RET_DK = 128
RET_DV = 256
RET_CHUNK = 128
FOX_HEADS = 8
FOX_DH = 64
PEER_HEADS = 8
PEER_NKEYS = 128
PEER_NEXP = PEER_NKEYS * PEER_NKEYS
PEER_HALF = 128
PEER_TOPK = 16
PLE_DIM = 256
LN_EPS = 1e-5
ROPE_BASE = 10000.0

RET_QK_W = RET_HEADS * RET_DK
RET_V_W = RET_HEADS * RET_DV
FOX_W = FOX_HEADS * FOX_DH

OFF_RQ, OFF_RK, OFF_RV, OFF_RG = 0, 512, 1024, 2048
OFF_GR, OFF_GF = 3072, 4096
OFF_FQ, OFF_FK, OFF_FV = 5120, 5632, 6144
PROJ_W = 6656
FF_W = 512

V7X_VMEM_LIMIT = 56 * 1024 * 1024
NEG_BIG = -3.0e38
MARK_BASE = -(2.0 ** 127)
CAND_CELLS = tuple((a, b) for a in range(PEER_TOPK) for b in range(PEER_TOPK)
                   if (a + 1) * (b + 1) <= PEER_TOPK)


def _layer_norm(x, g, b):
    mu = jnp.mean(x, axis=-1, keepdims=True)
    xc = x - mu
    var = jnp.mean(xc * xc, axis=-1, keepdims=True)
    return xc * lax.rsqrt(var + LN_EPS) * g + b


def _params(*sem):
    return pltpu.CompilerParams(dimension_semantics=sem, vmem_limit_bytes=V7X_VMEM_LIMIT)


def _inproj_kernel(x_ref, g_ref, b_ref, w_ref, wff_ref, o_ref, ff_ref, h_scr):
    @pl.when(pl.program_id(1) == 0)
    def _():
        hb = _layer_norm(x_ref[...], g_ref[...], b_ref[...]).astype(BF16)
        h_scr[...] = hb
        ff_ref[...] = jnp.dot(hb, wff_ref[...], preferred_element_type=F32)

    o_ref[...] = jnp.dot(h_scr[...], w_ref[...], preferred_element_type=F32).astype(BF16)


def _in_proj(x2, g, b, w_main, w_ff, tm=1024, tn=1664):
    t = x2.shape[0]
    return pl.pallas_call(
        _inproj_kernel,
        grid=(t // tm, PROJ_W // tn),
        in_specs=[
            pl.BlockSpec((tm, D_MODEL), lambda i, j: (i, 0)),
            pl.BlockSpec((1, D_MODEL), lambda i, j: (0, 0)),
            pl.BlockSpec((1, D_MODEL), lambda i, j: (0, 0)),
            pl.BlockSpec((D_MODEL, tn), lambda i, j: (0, j)),
            pl.BlockSpec((D_MODEL, FF_W), lambda i, j: (0, 0)),
        ],
        out_specs=[
            pl.BlockSpec((tm, tn), lambda i, j: (i, j)),
            pl.BlockSpec((tm, FF_W), lambda i, j: (i, 0)),
        ],
        out_shape=[jax.ShapeDtypeStruct((t, PROJ_W), BF16), jax.ShapeDtypeStruct((t, FF_W), F32)],
        scratch_shapes=[pltpu.VMEM((tm, D_MODEL), BF16)],
        compiler_params=_params("parallel", "arbitrary"),
        name="in_proj",
    )(x2, g, b, w_main, w_ff)


def _retention_kernel(q_ref, k_ref, v_ref, g_ref, cos_ref, sin_ref, intra_ref, qd_ref, kd_ref,
                      o_ref, r_scr, *, chunk_decay):
    @pl.when(pl.program_id(1) == 0)
    def _():
        r_scr[...] = jnp.zeros_like(r_scr)

    cos = cos_ref[...]
    sin = sin_ref[...]
    for h in range(RET_HEADS):
        q = q_ref[:, h * RET_DK:(h + 1) * RET_DK].astype(F32)
        k = k_ref[:, h * RET_DK:(h + 1) * RET_DK].astype(F32)
        v = v_ref[:, h * RET_DV:(h + 1) * RET_DV]
        qr = q * cos + pltpu.roll(q, RET_DK // 2, 1) * sin
        kr = (k * cos + pltpu.roll(k, RET_DK // 2, 1) * sin) * (RET_DK ** -0.5)
        s = lax.dot_general(qr.astype(BF16), kr.astype(BF16), (((1,), (1,)), ((), ())),
                            preferred_element_type=F32) * intra_ref[h]
        inner = jnp.dot(s.astype(BF16), v, preferred_element_type=F32)
        r_old = r_scr[h]
        cross = jnp.dot((qr * qd_ref[h]).astype(BF16), r_old.astype(BF16), preferred_element_type=F32)
        kv = lax.dot_general((kr * kd_ref[h]).astype(BF16), v, (((0,), (0,)), ((), ())),
                             preferred_element_type=F32)
        r_scr[h] = chunk_decay[h] * r_old + kv
        y = inner + cross
        mu = jnp.mean(y, axis=-1, keepdims=True)
        yc = y - mu
        var = jnp.mean(yc * yc, axis=-1, keepdims=True)
        yn = yc * lax.rsqrt(var + LN_EPS)
        gate = g_ref[:, h * RET_DV:(h + 1) * RET_DV].astype(F32)
        o_ref[:, h * RET_DV:(h + 1) * RET_DV] = (gate * jax.nn.sigmoid(gate) * yn).astype(BF16)


def _retention(proj, cos_t, sin_t, intra, qdec, kdec, chunk_decay, batch, seq):
    c = RET_CHUNK
    nc = seq // c
    row = lambda b, j: b * nc + j
    return pl.pallas_call(
        functools.partial(_retention_kernel, chunk_decay=chunk_decay),
        grid=(batch, nc),
        in_specs=[
            pl.BlockSpec((c, RET_QK_W), lambda b, j: (row(b, j), OFF_RQ // RET_QK_W)),
            pl.BlockSpec((c, RET_QK_W), lambda b, j: (row(b, j), OFF_RK // RET_QK_W)),
            pl.BlockSpec((c, RET_V_W), lambda b, j: (row(b, j), OFF_RV // RET_V_W)),
            pl.BlockSpec((c, RET_V_W), lambda b, j: (row(b, j), OFF_RG // RET_V_W)),
            pl.BlockSpec((c, RET_DK), lambda b, j: (j, 0)),
            pl.BlockSpec((c, RET_DK), lambda b, j: (j, 0)),
            pl.BlockSpec((RET_HEADS, c, c), lambda b, j: (0, 0, 0)),
            pl.BlockSpec((RET_HEADS, c, RET_DK), lambda b, j: (0, 0, 0)),
            pl.BlockSpec((RET_HEADS, c, RET_DK), lambda b, j: (0, 0, 0)),
        ],
        out_specs=pl.BlockSpec((c, RET_V_W), lambda b, j: (row(b, j), 0)),
        out_shape=jax.ShapeDtypeStruct((batch * seq, RET_V_W), BF16),
        scratch_shapes=[pltpu.VMEM((RET_HEADS, RET_DK, RET_DV), F32)],
        compiler_params=_params("parallel", "arbitrary"),
        name="retention",
    )(proj, proj, proj, proj, cos_t, sin_t, intra, qdec, kdec)


def _fcumsum_kernel(ff_ref, bf_ref, tri_ref, c_ref, ct_ref, *, blk):
    seq = ff_ref.shape[0]
    carry = jnp.zeros((1, FF_W), F32)
    for i in range(seq // blk):
        lf = jax.nn.log_sigmoid(ff_ref[i * blk:(i + 1) * blk, :] + bf_ref[...])
        cs = jnp.dot(tri_ref[...], lf, preferred_element_type=F32,
                     precision=lax.Precision.HIGHEST) + carry
        c_ref[i * blk:(i + 1) * blk, :] = cs
        carry = cs[blk - 1:blk, :]
        for p in range(FOX_HEADS // 2):
            ct_ref[0, p, i] = cs[:, p * 128:(p + 1) * 128].T[0:8, :]


def _fcumsum(ff, bf_pad, batch, seq, blk):
    tri = (lax.broadcasted_iota(jnp.int32, (blk, blk), 0)
           >= lax.broadcasted_iota(jnp.int32, (blk, blk), 1)).astype(F32)
    nb = seq // blk
    return pl.pallas_call(
        functools.partial(_fcumsum_kernel, blk=blk),
        grid=(batch,),
        in_specs=[
            pl.BlockSpec((seq, FF_W), lambda b: (b, 0)),
            pl.BlockSpec((1, FF_W), lambda b: (0, 0)),
            pl.BlockSpec((blk, blk), lambda b: (0, 0)),
        ],
        out_specs=[
            pl.BlockSpec((seq, FF_W), lambda b: (b, 0)),
            pl.BlockSpec((1, FOX_HEADS // 2, nb, 8, blk), lambda b: (b, 0, 0, 0, 0)),
        ],
        out_shape=[jax.ShapeDtypeStruct((batch * seq, FF_W), F32),
                   jax.ShapeDtypeStruct((batch, FOX_HEADS // 2, nb, 8, blk), F32)],
        compiler_params=_params("parallel"),
        name="fcumsum",
    )(ff, bf_pad, tri)


def _fox_kernel(q_ref, k_ref, v_ref, c_ref, ct_ref, o_ref, vt_scr, ckb_scr, m_scr, l_scr, acc_scr, *, tq):
    qi = pl.program_id(2)
    seq = k_ref.shape[0]
    nkv = seq // tq

    @pl.when(qi == 0)
    def _():
        for j in range(nkv):
            vt_scr[j] = v_ref[j * tq:(j + 1) * tq, :].astype(F32).T.astype(BF16)
        for r in range(2):
            ckb_scr[r] = jnp.broadcast_to(c_ref[:, r:r + 1], (seq, 128))

    lane = lax.broadcasted_iota(jnp.int32, (tq, 128), 1)
    key_i = lax.broadcasted_iota(jnp.int32, (tq, tq), 0)
    qry_i = lax.broadcasted_iota(jnp.int32, (tq, tq), 1)
    q = q_ref[...] * jnp.asarray(FOX_DH ** -0.5, BF16)
    qh = [jnp.where((lane >= r * FOX_DH) & (lane < (r + 1) * FOX_DH), q, jnp.zeros_like(q)) for r in range(2)]
    cq = [ct_ref[0, 0, qi, r:r + 1, :] for r in range(2)]
    m_scr[...] = jnp.full_like(m_scr, NEG_BIG)
    l_scr[...] = jnp.zeros_like(l_scr)
    acc_scr[...] = jnp.zeros_like(acc_scr)

    def block(j, diag):
        start = pl.multiple_of(j * tq, tq)
        kb = k_ref[pl.ds(start, tq), :]
        vtb = vt_scr[j]
        s = [lax.dot_general(kb, qh[r], (((1,), (1,)), ((), ())), preferred_element_type=F32)
             for r in range(2)]
        ck = [ckb_scr[r, pl.ds(start, tq), :] for r in range(2)]
        s = [s[r] + cq[r] - jnp.concatenate([ck[r]] * (tq // 128), axis=1) for r in range(2)]
        if diag:
            s = [jnp.where(key_i <= qry_i, s[r], -1e30) for r in range(2)]
        m_old = [m_scr[r] for r in range(2)]
        m_new = [jnp.maximum(m_old[r], jnp.max(s[r], axis=0, keepdims=True)) for r in range(2)]
        alpha = [jnp.exp(m_old[r] - m_new[r]) for r in range(2)]
        p = [jnp.exp(s[r] - m_new[r]) for r in range(2)]
        for r in range(2):
            l_scr[r] = alpha[r] * l_scr[r] + jnp.sum(p[r], axis=0, keepdims=True)
            pv = jnp.dot(vtb[r * FOX_DH:(r + 1) * FOX_DH, :], p[r].astype(BF16), preferred_element_type=F32)
            acc_scr[r] = alpha[r] * acc_scr[r] + pv
            m_scr[r] = m_new[r]

    def body(j, carry):
        block(j, False)
        return carry

    lax.fori_loop(0, qi, body, 0)
    block(qi, True)
    out_t = jnp.concatenate([acc_scr[r] / l_scr[r] for r in range(2)], axis=0)
    o_ref[...] = out_t.T.astype(BF16)


def _fox(proj, c, ct, batch, seq, tq):
    nq = seq // tq
    np_ = FOX_HEADS // 2
    return pl.pallas_call(
        functools.partial(_fox_kernel, tq=tq),
        grid=(batch, np_, nq),
        in_specs=[
            pl.BlockSpec((tq, 128), lambda b, p, i: (b * nq + i, OFF_FQ // 128 + p)),
            pl.BlockSpec((seq, 128), lambda b, p, i: (b, OFF_FK // 128 + p)),
            pl.BlockSpec((seq, 128), lambda b, p, i: (b, OFF_FV // 128 + p)),
            pl.BlockSpec((seq, 128), lambda b, p, i: (b, p)),
            pl.BlockSpec((1, 1, nq, 8, tq), lambda b, p, i: (b, p, 0, 0, 0)),
        ],
        out_specs=pl.BlockSpec((tq, 128), lambda b, p, i: (b * nq + i, p)),
        out_shape=jax.ShapeDtypeStruct((batch * seq, FOX_W), BF16),
        scratch_shapes=[
            pltpu.VMEM((nq, 128, tq), BF16),
            pltpu.VMEM((2, seq, 128), F32),
            pltpu.VMEM((2, 1, tq), F32),
            pltpu.VMEM((2, 1, tq), F32),
            pltpu.VMEM((2, FOX_DH, tq), F32),
        ],
        compiler_params=_params("parallel", "parallel", "arbitrary"),
        name="fox",
    )(proj, proj, proj, c, ct)


def _mix_kernel(x_ref, yr_ref, yf_ref, gr_ref, gf_ref, p_ref, ge_ref, be_ref, bg_ref, wro_ref, wfo_ref,
                wout_ref, g1_ref, b1_ref, wpg_ref, bpg_ref, wple_ref, base_ref, ht_ref, *, alpha):
    h0 = _layer_norm(x_ref[...], ge_ref[...], be_ref[...])
    y_ret = jnp.dot(yr_ref[...], wro_ref[...], preferred_element_type=F32)
    y_fox = jnp.dot(yf_ref[...], wfo_ref[...], preferred_element_type=F32)
    merged = (jax.nn.sigmoid(gr_ref[...].astype(F32) + bg_ref[0:1, :]) * y_ret
              + jax.nn.sigmoid(gf_ref[...].astype(F32) + bg_ref[1:2, :]) * y_fox)
    out = jnp.dot(merged.astype(BF16), wout_ref[...], preferred_element_type=F32)
    h1 = _layer_norm(alpha * h0 + out, g1_ref[...], b1_ref[...])
    h1b = h1.astype(BF16)
    gate = jax.nn.sigmoid(jnp.dot(h1b, wpg_ref[...], preferred_element_type=F32) + bpg_ref[...])
    ple = gate * jnp.dot(p_ref[...].astype(BF16), wple_ref[...], preferred_element_type=F32)
    base_ref[...] = alpha * h1 + ple
    ht_ref[...] = h1.T.astype(BF16)


def _mix(x2, yr, yf, proj, p2, ge, be, bg, wro, wfo, wout, g1, b1, wpg, bpg, wple, alpha, tm=512):
    t = x2.shape[0]
    full = lambda shape: pl.BlockSpec(shape, lambda i: (0,) * len(shape))
    return pl.pallas_call(
        functools.partial(_mix_kernel, alpha=alpha),
        grid=(t // tm,),
        in_specs=[
            pl.BlockSpec((tm, D_MODEL), lambda i: (i, 0)),
            pl.BlockSpec((tm, RET_V_W), lambda i: (i, 0)),
            pl.BlockSpec((tm, FOX_W), lambda i: (i, 0)),
            pl.BlockSpec((tm, D_MODEL), lambda i: (i, OFF_GR // D_MODEL)),
            pl.BlockSpec((tm, D_MODEL), lambda i: (i, OFF_GF // D_MODEL)),
            pl.BlockSpec((tm, PLE_DIM), lambda i: (i, 0)),
            full((1, D_MODEL)), full((1, D_MODEL)), full((2, D_MODEL)),
            full((RET_V_W, D_MODEL)), full((FOX_W, D_MODEL)), full((D_MODEL, D_MODEL)),
            full((1, D_MODEL)), full((1, D_MODEL)),
            full((D_MODEL, D_MODEL)), full((1, D_MODEL)), full((PLE_DIM, D_MODEL)),
        ],
        out_specs=[
            pl.BlockSpec((tm, D_MODEL), lambda i: (i, 0)),
            pl.BlockSpec((D_MODEL, tm), lambda i: (0, i)),
        ],
        out_shape=[jax.ShapeDtypeStruct((t, D_MODEL), F32), jax.ShapeDtypeStruct((D_MODEL, t), BF16)],
        compiler_params=_params("parallel"),
        name="mix",
    )(x2, yr, yf, proj, proj, p2, ge, be, bg, wro, wfo, wout, g1, b1, wpg, bpg, wple)


def _top16(sc):
    nk = sc.shape[0]
    iota = lax.broadcasted_iota(jnp.int32, sc.shape, 0).astype(F32)
    rank = jnp.full(sc.shape, float(PEER_TOPK), F32)
    vals = []
    for a in range(PEER_TOPK):
        m = jnp.max(sc, axis=0, keepdims=True)
        first = jnp.min(jnp.where(sc == m, iota, float(nk)), axis=0, keepdims=True)
        hit = iota == first
        sc = jnp.where(hit, NEG_BIG, sc)
        rank = jnp.where(hit, float(a), rank)
        vals.append(m)
    return vals, rank


def _mark(a):
    return MARK_BASE * (1.0 + (a + 1) / 64.0)


def _top16_no_ties(chains):
    vals = [[] for _ in chains]
    for a in range(PEER_TOPK):
        ms = [jnp.max(sc, axis=0, keepdims=True) for sc in chains]
        chains = [jnp.where(sc == m, _mark(a), sc) for sc, m in zip(chains, ms)]
        for v, m in zip(vals, ms):
            v.append(m)
    return vals, chains


def _route_kernel(ht_ref, wq_ref, keys_ref, ct_ref, w0_ref, r1_ref, w1_ref,
                  q_scr, sc_scr, rank_scr, val_scr, cnt_scr, z_scr, *, lanes):
    tb = ht_ref.shape[1]
    groups = [slice(g * lanes, (g + 1) * lanes) for g in range(tb // lanes)]
    q_scr[...] = jnp.dot(wq_ref[...], ht_ref[...], preferred_element_type=F32).astype(BF16)
    val_scr[...] = jnp.zeros_like(val_scr)

    for hc in range(2 * PEER_HEADS):
        sc_scr[hc] = jnp.dot(keys_ref[hc], q_scr[hc * PEER_HALF:(hc + 1) * PEER_HALF, :],
                             preferred_element_type=F32)

    def store_vals(c, h, sl, vals):
        own_row = lax.broadcasted_iota(jnp.int32, (PEER_HEADS, lanes), 0) == h
        for a in range(PEER_TOPK):
            val_scr[c, a, :, sl] = jnp.where(own_row, vals[a], val_scr[c, a, :, sl])

    def select_cells(exact):
        v0 = [val_scr[0, a] for a in range(PEER_TOPK)]
        v1 = [val_scr[1, b] for b in range(PEER_TOPK)]
        cand = [v0[a] + v1[b] for a, b in CAND_CELLS]
        flat = [float(a * PEER_TOPK + b) for a, b in CAND_CELLS]
        sel = [jnp.zeros_like(v0[0]) for _ in CAND_CELLS]
        for _ in range(PEER_TOPK):
            m = functools.reduce(jnp.maximum, cand)
            if exact:
                first = functools.reduce(jnp.minimum,
                                         [jnp.where(cv == m, fi, 1e9) for cv, fi in zip(cand, flat)])
                hits = [first == fi for fi in flat]
            else:
                hits = [cv == m for cv in cand]
            cand = [jnp.where(hit, NEG_BIG, cv) for hit, cv in zip(hits, cand)]
            sel = [jnp.where(hit, 1.0, sv) for hit, sv in zip(hits, sel)]
        e0 = [jnp.exp(v0[a] - v0[0]) for a in range(PEER_TOPK)]
        e1 = [jnp.exp(v1[b] - v1[0]) for b in range(PEER_TOPK)]
        z = jnp.zeros_like(v0[0])
        cnt = [jnp.zeros_like(v0[0]) for _ in range(PEER_TOPK)]
        for t, (a, b) in enumerate(CAND_CELLS):
            z = z + sel[t] * (e0[a] * e1[b])
            cnt[a] = cnt[a] + sel[t]
        z_scr[...] = 1.0 / z
        for a in range(PEER_TOPK):
            cnt_scr[a] = cnt[a]
        return functools.reduce(jnp.add, sel)

    def fast_head(h, bad):
        for sl in groups:
            vals, marked = _top16_no_ties([sc_scr[2 * h, :, sl], sc_scr[2 * h + 1, :, sl]])
            for c in range(2):
                taken = marked[c] <= MARK_BASE
                unit = marked[c] * -(2.0 ** -100) * (2.0 ** -27)
                rank_scr[2 * h + c, :, sl] = jnp.where(taken, (unit - 1.0) * 64.0 - 1.0, float(PEER_TOPK))
                n_taken = jnp.sum(jnp.where(taken, 1.0, 0.0), axis=0, keepdims=True)
                bad = jnp.maximum(bad, jnp.abs(n_taken - float(PEER_TOPK)))
                store_vals(c, h, sl, vals[c])
        return bad

    bad = jnp.max(lax.fori_loop(0, PEER_HEADS, fast_head, jnp.zeros((1, lanes), F32)))
    bad = jnp.maximum(bad, jnp.max(jnp.abs(select_cells(False) - float(PEER_TOPK))))

    @pl.when(bad > 0.0)
    def _():
        def exact_vec(hc, carry):
            for sl in groups:
                vals, rank = _top16(sc_scr[hc, :, sl])
                rank_scr[hc, :, sl] = rank
                store_vals(hc % 2, hc // 2, sl, vals)
            return carry

        lax.fori_loop(0, 2 * PEER_HEADS, exact_vec, 0)
        select_cells(True)

    for h in range(PEER_HEADS):
        s0 = sc_scr[2 * h]
        s1 = sc_scr[2 * h + 1]
        rank0 = rank_scr[2 * h]
        ct = jnp.zeros_like(s0)
        for a in range(PEER_TOPK):
            ct = jnp.where(rank0 == float(a), cnt_scr[a, h:h + 1, :], ct)
        ct_ref[h] = ct
        w0_ref[h] = jnp.exp(s0 - val_scr[0, 0, h:h + 1, :]) * z_scr[h:h + 1, :]
        w1_ref[h] = jnp.exp(s1 - val_scr[1, 0, h:h + 1, :]).astype(BF16)
        r1_ref[h] = rank_scr[2 * h + 1].astype(BF16)


def _route(ht, wq_t, keys, tb=256, lanes=128):
    t = ht.shape[1]
    nq = 2 * PEER_HEADS * PEER_HALF
    routed = lambda dt: jax.ShapeDtypeStruct((PEER_HEADS, PEER_NKEYS, t), dt)
    rspec = pl.BlockSpec((PEER_HEADS, PEER_NKEYS, tb), lambda n: (0, 0, n))
    return pl.pallas_call(
        functools.partial(_route_kernel, lanes=lanes),
        grid=(t // tb,),
        in_specs=[
            pl.BlockSpec((D_MODEL, tb), lambda n: (0, n)),
            pl.BlockSpec((nq, D_MODEL), lambda n: (0, 0)),
            pl.BlockSpec((2 * PEER_HEADS, PEER_NKEYS, PEER_HALF), lambda n: (0, 0, 0)),
        ],
        out_specs=[rspec, rspec, rspec, rspec],
        out_shape=[routed(F32), routed(F32), routed(BF16), routed(BF16)],
        scratch_shapes=[
            pltpu.VMEM((nq, tb), BF16),
            pltpu.VMEM((2 * PEER_HEADS, PEER_NKEYS, tb), F32),
            pltpu.VMEM((2 * PEER_HEADS, PEER_NKEYS, tb), F32),
            pltpu.VMEM((2, PEER_TOPK, PEER_HEADS, tb), F32),
            pltpu.VMEM((PEER_TOPK, PEER_HEADS, tb), F32),
            pltpu.VMEM((PEER_HEADS, tb), F32),
        ],
        compiler_params=_params("parallel"),
        name="route",
    )(ht, wq_t, keys)


def _peer_kernel(ht_ref, u_ref, vt_ref, ct_ref, w0_ref, r1_ref, w1_ref, base_ref, g2_ref, b2_ref,
                 o_ref, hid_scr, a_scr, acc_scr, r1_scr, w1_scr, *, eb, lanes):
    e = pl.program_id(1)
    tb = ht_ref.shape[1]
    sub = 16

    @pl.when(e == 0)
    def _():
        acc_scr[...] = jnp.zeros_like(acc_scr)
        r1_scr[...] = r1_ref[...]
        w1_scr[...] = w1_ref[...]

    hid_scr[...] = jnp.dot(u_ref[...], ht_ref[...], preferred_element_type=F32)
    for il in range(eb // PEER_NKEYS):
        for lg in range(tb // lanes):
            sl = slice(lg * lanes, (lg + 1) * lanes)
            cts = [jnp.broadcast_to(ct_ref[h, il:il + 1, sl], (sub, lanes)).astype(BF16)
                   for h in range(PEER_HEADS)]
            w0s = [jnp.broadcast_to(w0_ref[h, il:il + 1, sl], (sub, lanes)).astype(BF16)
                   for h in range(PEER_HEADS)]
            for jb in range(PEER_NKEYS // sub):
                js = slice(jb * sub, (jb + 1) * sub)
                gsum = None
                for h in range(PEER_HEADS):
                    w1 = w1_scr[h, js, sl]
                    term = jnp.where(r1_scr[h, js, sl] < cts[h], w1, jnp.zeros_like(w1)) * w0s[h]
                    gsum = term if gsum is None else gsum + term
                rows = slice(il * PEER_NKEYS + jb * sub, il * PEER_NKEYS + (jb + 1) * sub)
                hv = hid_scr[rows, sl]
                act = 0.5 * hv * (1.0 + lax.erf(hv * (1.0 / math.sqrt(2.0))))
                a_scr[rows, sl] = act.astype(BF16) * gsum
    acc_scr[...] += jnp.dot(vt_ref[...], a_scr[...], preferred_element_type=F32)

    @pl.when(e == pl.num_programs(1) - 1)
    def _():
        o_ref[...] = _layer_norm(base_ref[...] + acc_scr[...].T, g2_ref[...], b2_ref[...])


def _peer(ht, u_bf, vt_bf, ct, w0, r1, w1, base, g2, b2, tb=512, eb=1024, lanes=128):
    t = ht.shape[1]
    kr = eb // PEER_NKEYS
    rspec = pl.BlockSpec((PEER_HEADS, PEER_NKEYS, tb), lambda n, e: (0, 0, n))
    kspec = pl.BlockSpec((PEER_HEADS, kr, tb), lambda n, e: (0, e, n))
    return pl.pallas_call(
        functools.partial(_peer_kernel, eb=eb, lanes=lanes),
        grid=(t // tb, PEER_NEXP // eb),
        in_specs=[
            pl.BlockSpec((D_MODEL, tb), lambda n, e: (0, n)),
            pl.BlockSpec((eb, D_MODEL), lambda n, e: (e, 0)),
            pl.BlockSpec((D_MODEL, eb), lambda n, e: (0, e)),
            kspec, kspec, rspec, rspec,
            pl.BlockSpec((tb, D_MODEL), lambda n, e: (n, 0)),
            pl.BlockSpec((1, D_MODEL), lambda n, e: (0, 0)),
            pl.BlockSpec((1, D_MODEL), lambda n, e: (0, 0)),
        ],
        out_specs=pl.BlockSpec((tb, D_MODEL), lambda n, e: (n, 0)),
        out_shape=jax.ShapeDtypeStruct((t, D_MODEL), F32),
        scratch_shapes=[
            pltpu.VMEM((eb, tb), F32),
            pltpu.VMEM((eb, tb), BF16),
            pltpu.VMEM((D_MODEL, tb), F32),
            pltpu.VMEM((PEER_HEADS, PEER_NKEYS, tb), BF16),
            pltpu.VMEM((PEER_HEADS, PEER_NKEYS, tb), BF16),
        ],
        compiler_params=_params("parallel", "arbitrary"),
        name="peer",
    )(ht, u_bf, vt_bf, ct, w0, r1, w1, base, g2, b2)


def _retention_tables(seq):
    half = RET_DK // 2
    inv = ROPE_BASE ** (-jnp.arange(half, dtype=F32) / half)
    ang = jnp.arange(seq, dtype=jnp.int32).astype(F32)[:, None] * inv[None, :]
    cos, sin = jnp.cos(ang), jnp.sin(ang)
    cos_t = jnp.concatenate([cos, cos], axis=-1)
    sin_t = jnp.concatenate([-sin, sin], axis=-1)
    c = RET_CHUNK
    log_g = jnp.log(1.0 - 2.0 ** (-5.0 - jnp.arange(RET_HEADS, dtype=F32)))
    idx = jnp.arange(c, dtype=F32)
    diff = idx[:, None] - idx[None, :]
    intra = jnp.where(diff >= 0, jnp.exp(log_g[:, None, None] * jnp.maximum(diff, 0.0)), 0.0)
    qdec = jnp.broadcast_to(jnp.exp(log_g[:, None] * (idx + 1.0))[..., None], (RET_HEADS, c, RET_DK))
    kdec = jnp.broadcast_to(jnp.exp(log_g[:, None] * (c - 1.0 - idx))[..., None], (RET_HEADS, c, RET_DK))
    chunk_decay = tuple(math.exp(math.log(1.0 - 2.0 ** (-5.0 - h)) * c) for h in range(RET_HEADS))
    return cos_t, sin_t, intra, qdec, kdec, chunk_decay


def _pack_w_in(w):
    rq, rk, rv, rg, fq, fk, fv, ff, gr, gf = jnp.split(
        w, [512, 1024, 2048, 3072, 3584, 4096, 4608, 4616, 5640], axis=-1)
    main = jnp.concatenate([rq, rk, rv, rg, gr, gf, fq, fk, fv], axis=-1).astype(BF16)
    ff_pad = jnp.zeros((w.shape[0], FOX_HEADS // 2, 128), w.dtype).at[:, :, 0:2].set(
        ff.reshape(w.shape[0], FOX_HEADS // 2, 2)).reshape(w.shape[0], FF_W).astype(BF16)
    return main, ff_pad


def kernel(x, p, ln_emb_g, ln_emb_b, w_in, b_forget, b_branch_gate, w_ret_o, w_fox_o, w_out, ln1_g, ln1_b,
           w_peer_q, peer_sub_keys, peer_u, peer_v, w_ple_gate, b_ple_gate, w_ple, ln2_g, ln2_b):
    batch, seq, d = x.shape
    depth = w_in.shape[0]
    assert depth == 1 and d == D_MODEL, "the trunk-entry norm is fused into the single layer"
    t = batch * seq
    alpha = (2.0 * depth) ** 0.25
    fox_tq = min(512, seq)
    row = lambda v: v.reshape(1, -1).astype(F32)
    cos_t, sin_t, intra, qdec, kdec, chunk_decay = _retention_tables(seq)

    x2 = x.reshape(t, d)
    ge, be = row(ln_emb_g), row(ln_emb_b)
    w_main, w_ff = _pack_w_in(w_in[0])
    bf_pad = jnp.zeros((FOX_HEADS // 2, 128), F32).at[:, 0:2].set(
        b_forget[0].astype(F32).reshape(FOX_HEADS // 2, 2)).reshape(1, FF_W)
    proj, ff = _in_proj(x2, ge, be, w_main, w_ff)
    yr = _retention(proj, cos_t, sin_t, intra, qdec, kdec, chunk_decay, batch, seq)
    c, ct = _fcumsum(ff, bf_pad, batch, seq, fox_tq)
    yf = _fox(proj, c, ct, batch, seq, fox_tq)
    base, ht = _mix(x2, yr, yf, proj, p[0].reshape(t, PLE_DIM), ge, be, b_branch_gate[0].astype(F32),
                    w_ret_o[0].astype(BF16), w_fox_o[0].astype(BF16), w_out[0].astype(BF16),
                    row(ln1_g[0]), row(ln1_b[0]), w_ple_gate[0].astype(BF16), row(b_ple_gate[0]),
                    w_ple[0].astype(BF16), alpha)
    keys = peer_sub_keys[0].reshape(2 * PEER_HEADS, PEER_NKEYS, PEER_HALF).astype(BF16)
    ct_r, w0, r1, w1 = _route(ht, w_peer_q[0].T.astype(BF16), keys)
    out = _peer(ht, peer_u[0].astype(BF16), peer_v[0].T.astype(BF16), ct_r, w0, r1, w1, base,
                row(ln2_g[0]), row(ln2_b[0]))
    return out.reshape(batch, seq, d)
```

```python
import functools
import math

import jax
import jax.numpy as jnp
from jax import lax
from jax.experimental import pallas as pl
from jax.experimental.pallas import tpu as pltpu

F32 = jnp.float32
BF16 = jnp.bfloat16

D_MODEL = 1024
RET_HEADS = 4
RET_DK = 128
RET_DV = 256
RET_CHUNK = 128
FOX_HEADS = 8
FOX_DH = 64
PEER_HEADS = 8
PEER_NKEYS = 128
PEER_NEXP = PEER_NKEYS * PEER_NKEYS
PEER_HALF = 128
PEER_TOPK = 16
PLE_DIM = 256
LN_EPS = 1e-5
ROPE_BASE = 10000.0

RET_QK_W = RET_HEADS * RET_DK
RET_V_W = RET_HEADS * RET_DV
FOX_W = FOX_HEADS * FOX_DH

OFF_RQ, OFF_RK, OFF_RV, OFF_RG = 0, 512, 1024, 2048
OFF_GR, OFF_GF = 3072, 4096
OFF_FQ, OFF_FK, OFF_FV = 5120, 5632, 6144
PROJ_W = 6656
FF_W = 512

V7X_VMEM_LIMIT = 56 * 1024 * 1024
NEG_BIG = -3.0e38
MARK_BASE = -(2.0 ** 127)
CAND_CELLS = tuple((a, b) for a in range(PEER_TOPK) for b in range(PEER_TOPK)
                   if (a + 1) * (b + 1) <= PEER_TOPK)


def _layer_norm(x, g, b):
    mu = jnp.mean(x, axis=-1, keepdims=True)
    xc = x - mu
    var = jnp.mean(xc * xc, axis=-1, keepdims=True)
    return xc * lax.rsqrt(var + LN_EPS) * g + b


def _params(*sem):
    return pltpu.CompilerParams(dimension_semantics=sem, vmem_limit_bytes=V7X_VMEM_LIMIT)


def _inproj_kernel(x_ref, g_ref, b_ref, w_ref, wff_ref, o_ref, ff_ref, h_scr):
    @pl.when(pl.program_id(1) == 0)
    def _():
        hb = _layer_norm(x_ref[...], g_ref[...], b_ref[...]).astype(BF16)
        h_scr[...] = hb
        ff_ref[...] = jnp.dot(hb, wff_ref[...], preferred_element_type=F32)

    o_ref[...] = jnp.dot(h_scr[...], w_ref[...], preferred_element_type=F32).astype(BF16)


def _in_proj(x2, g, b, w_main, w_ff, tm=1024, tn=1664):
    t = x2.shape[0]
    return pl.pallas_call(
        _inproj_kernel,
        grid=(t // tm, PROJ_W // tn),
        in_specs=[
            pl.BlockSpec((tm, D_MODEL), lambda i, j: (i, 0)),
            pl.BlockSpec((1, D_MODEL), lambda i, j: (0, 0)),
            pl.BlockSpec((1, D_MODEL), lambda i, j: (0, 0)),
            pl.BlockSpec((D_MODEL, tn), lambda i, j: (0, j)),
            pl.BlockSpec((D_MODEL, FF_W), lambda i, j: (0, 0)),
        ],
        out_specs=[
            pl.BlockSpec((tm, tn), lambda i, j: (i, j)),
            pl.BlockSpec((tm, FF_W), lambda i, j: (i, 0)),
        ],
        out_shape=[jax.ShapeDtypeStruct((t, PROJ_W), BF16), jax.ShapeDtypeStruct((t, FF_W), F32)],
        scratch_shapes=[pltpu.VMEM((tm, D_MODEL), BF16)],
        compiler_params=_params("parallel", "arbitrary"),
        name="in_proj",
    )(x2, g, b, w_main, w_ff)


def _retention_kernel(q_ref, k_ref, v_ref, g_ref, cos_ref, sin_ref, intra_ref, qd_ref, kd_ref,
                      o_ref, r_scr, *, chunk_decay):
    @pl.when(pl.program_id(1) == 0)
    def _():
        r_scr[...] = jnp.zeros_like(r_scr)

    cos = cos_ref[...]
    sin = sin_ref[...]
    for h in range(RET_HEADS):
        q = q_ref[:, h * RET_DK:(h + 1) * RET_DK].astype(F32)
        k = k_ref[:, h * RET_DK:(h + 1) * RET_DK].astype(F32)
        v = v_ref[:, h * RET_DV:(h + 1) * RET_DV]
        qr = q * cos + pltpu.roll(q, RET_DK // 2, 1) * sin
        kr = (k * cos + pltpu.roll(k, RET_DK // 2, 1) * sin) * (RET_DK ** -0.5)
        s = lax.dot_general(qr.astype(BF16), kr.astype(BF16), (((1,), (1,)), ((), ())),
                            preferred_element_type=F32) * intra_ref[h]
        inner = jnp.dot(s.astype(BF16), v, preferred_element_type=F32)
        r_old = r_scr[h]
        cross = jnp.dot((qr * qd_ref[h]).astype(BF16), r_old.astype(BF16), preferred_element_type=F32)
        kv = lax.dot_general((kr * kd_ref[h]).astype(BF16), v, (((0,), (0,)), ((), ())),
                             preferred_element_type=F32)
        r_scr[h] = chunk_decay[h] * r_old + kv
        y = inner + cross
        mu = jnp.mean(y, axis=-1, keepdims=True)
        yc = y - mu
        var = jnp.mean(yc * yc, axis=-1, keepdims=True)
        yn = yc * lax.rsqrt(var + LN_EPS)
        gate = g_ref[:, h * RET_DV:(h + 1) * RET_DV].astype(F32)
        o_ref[:, h * RET_DV:(h + 1) * RET_DV] = (gate * jax.nn.sigmoid(gate) * yn).astype(BF16)


def _retention(proj, cos_t, sin_t, intra, qdec, kdec, chunk_decay, batch, seq):
    c = RET_CHUNK
    nc = seq // c
    row = lambda b, j: b * nc + j
    return pl.pallas_call(
        functools.partial(_retention_kernel, chunk_decay=chunk_decay),
        grid=(batch, nc),
        in_specs=[
            pl.BlockSpec((c, RET_QK_W), lambda b, j: (row(b, j), OFF_RQ // RET_QK_W)),
            pl.BlockSpec((c, RET_QK_W), lambda b, j: (row(b, j), OFF_RK // RET_QK_W)),
            pl.BlockSpec((c, RET_V_W), lambda b, j: (row(b, j), OFF_RV // RET_V_W)),
            pl.BlockSpec((c, RET_V_W), lambda b, j: (row(b, j), OFF_RG // RET_V_W)),
            pl.BlockSpec((c, RET_DK), lambda b, j: (j, 0)),
            pl.BlockSpec((c, RET_DK), lambda b, j: (j, 0)),
            pl.BlockSpec((RET_HEADS, c, c), lambda b, j: (0, 0, 0)),
            pl.BlockSpec((RET_HEADS, c, RET_DK), lambda b, j: (0, 0, 0)),
            pl.BlockSpec((RET_HEADS, c, RET_DK), lambda b, j: (0, 0, 0)),
        ],
        out_specs=pl.BlockSpec((c, RET_V_W), lambda b, j: (row(b, j), 0)),
        out_shape=jax.ShapeDtypeStruct((batch * seq, RET_V_W), BF16),
        scratch_shapes=[pltpu.VMEM((RET_HEADS, RET_DK, RET_DV), F32)],
        compiler_params=_params("parallel", "arbitrary"),
        name="retention",
    )(proj, proj, proj, proj, cos_t, sin_t, intra, qdec, kdec)


def _fcumsum_kernel(ff_ref, bf_ref, tri_ref, c_ref, ct_ref, *, blk):
    seq = ff_ref.shape[0]
    carry = jnp.zeros((1, FF_W), F32)
    for i in range(seq // blk):
        lf = jax.nn.log_sigmoid(ff_ref[i * blk:(i + 1) * blk, :] + bf_ref[...])
        cs = jnp.dot(tri_ref[...], lf, preferred_element_type=F32,
                     precision=lax.Precision.HIGHEST) + carry
        c_ref[i * blk:(i + 1) * blk, :] = cs
        carry = cs[blk - 1:blk, :]
        for p in range(FOX_HEADS // 2):
            ct_ref[0, p, i] = cs[:, p * 128:(p + 1) * 128].T[0:8, :]


def _fcumsum(ff, bf_pad, batch, seq, blk):
    tri = (lax.broadcasted_iota(jnp.int32, (blk, blk), 0)
           >= lax.broadcasted_iota(jnp.int32, (blk, blk), 1)).astype(F32)
    nb = seq // blk
    return pl.pallas_call(
        functools.partial(_fcumsum_kernel, blk=blk),
        grid=(batch,),
        in_specs=[
            pl.BlockSpec((seq, FF_W), lambda b: (b, 0)),
            pl.BlockSpec((1, FF_W), lambda b: (0, 0)),
            pl.BlockSpec((blk, blk), lambda b: (0, 0)),
        ],
        out_specs=[
            pl.BlockSpec((seq, FF_W), lambda b: (b, 0)),
            pl.BlockSpec((1, FOX_HEADS // 2, nb, 8, blk), lambda b: (b, 0, 0, 0, 0)),
        ],
        out_shape=[jax.ShapeDtypeStruct((batch * seq, FF_W), F32),
                   jax.ShapeDtypeStruct((batch, FOX_HEADS // 2, nb, 8, blk), F32)],
        compiler_params=_params("parallel"),
        name="fcumsum",
    )(ff, bf_pad, tri)


def _fox_kernel(q_ref, k_ref, v_ref, c_ref, ct_ref, o_ref, vt_scr, ckb_scr, m_scr, l_scr, acc_scr, *, tq):
    qi = pl.program_id(2)
    seq = k_ref.shape[0]
    nkv = seq // tq

    @pl.when(qi == 0)
    def _():
        for j in range(nkv):
            vt_scr[j] = v_ref[j * tq:(j + 1) * tq, :].astype(F32).T.astype(BF16)
        for r in range(2):
            ckb_scr[r] = jnp.broadcast_to(c_ref[:, r:r + 1], (seq, 128))

    lane = lax.broadcasted_iota(jnp.int32, (tq, 128), 1)
    key_i = lax.broadcasted_iota(jnp.int32, (tq, tq), 0)
    qry_i = lax.broadcasted_iota(jnp.int32, (tq, tq), 1)
    q = q_ref[...] * jnp.asarray(FOX_DH ** -0.5, BF16)
    qh = [jnp.where((lane >= r * FOX_DH) & (lane < (r + 1) * FOX_DH), q, jnp.zeros_like(q)) for r in range(2)]
    cq = [ct_ref[0, 0, qi, r:r + 1, :] for r in range(2)]
    m_scr[...] = jnp.full_like(m_scr, NEG_BIG)
    l_scr[...] = jnp.zeros_like(l_scr)
    acc_scr[...] = jnp.zeros_like(acc_scr)

    def block(j, diag):
        start = pl.multiple_of(j * tq, tq)
        kb = k_ref[pl.ds(start, tq), :]
        vtb = vt_scr[j]
        s = [lax.dot_general(kb, qh[r], (((1,), (1,)), ((), ())), preferred_element_type=F32)
             for r in range(2)]
        ck = [ckb_scr[r, pl.ds(start, tq), :] for r in range(2)]
        s = [s[r] + cq[r] - jnp.concatenate([ck[r]] * (tq // 128), axis=1) for r in range(2)]
        if diag:
            s = [jnp.where(key_i <= qry_i, s[r], -1e30) for r in range(2)]
        m_old = [m_scr[r] for r in range(2)]
        m_new = [jnp.maximum(m_old[r], jnp.max(s[r], axis=0, keepdims=True)) for r in range(2)]
        alpha = [jnp.exp(m_old[r] - m_new[r]) for r in range(2)]
        p = [jnp.exp(s[r] - m_new[r]) for r in range(2)]
        for r in range(2):
            l_scr[r] = alpha[r] * l_scr[r] + jnp.sum(p[r], axis=0, keepdims=True)
            pv = jnp.dot(vtb[r * FOX_DH:(r + 1) * FOX_DH, :], p[r].astype(BF16), preferred_element_type=F32)
            acc_scr[r] = alpha[r] * acc_scr[r] + pv
            m_scr[r] = m_new[r]

    def body(j, carry):
        block(j, False)
        return carry

    lax.fori_loop(0, qi, body, 0)
    block(qi, True)
    out_t = jnp.concatenate([acc_scr[r] / l_scr[r] for r in range(2)], axis=0)
    o_ref[...] = out_t.T.astype(BF16)


def _fox(proj, c, ct, batch, seq, tq):
    nq = seq // tq
    np_ = FOX_HEADS // 2
    return pl.pallas_call(
        functools.partial(_fox_kernel, tq=tq),
        grid=(batch, np_, nq),
        in_specs=[
            pl.BlockSpec((tq, 128), lambda b, p, i: (b * nq + i, OFF_FQ // 128 + p)),
            pl.BlockSpec((seq, 128), lambda b, p, i: (b, OFF_FK // 128 + p)),
            pl.BlockSpec((seq, 128), lambda b, p, i: (b, OFF_FV // 128 + p)),
            pl.BlockSpec((seq, 128), lambda b, p, i: (b, p)),
            pl.BlockSpec((1, 1, nq, 8, tq), lambda b, p, i: (b, p, 0, 0, 0)),
        ],
        out_specs=pl.BlockSpec((tq, 128), lambda b, p, i: (b * nq + i, p)),
        out_shape=jax.ShapeDtypeStruct((batch * seq, FOX_W), BF16),
        scratch_shapes=[
            pltpu.VMEM((nq, 128, tq), BF16),
            pltpu.VMEM((2, seq, 128), F32),
            pltpu.VMEM((2, 1, tq), F32),
            pltpu.VMEM((2, 1, tq), F32),
            pltpu.VMEM((2, FOX_DH, tq), F32),
        ],
        compiler_params=_params("parallel", "parallel", "arbitrary"),
        name="fox",
    )(proj, proj, proj, c, ct)


def _mix_kernel(x_ref, yr_ref, yf_ref, gr_ref, gf_ref, p_ref, ge_ref, be_ref, bg_ref, wro_ref, wfo_ref,
                wout_ref, g1_ref, b1_ref, wpg_ref, bpg_ref, wple_ref, base_ref, ht_ref, *, alpha):
    h0 = _layer_norm(x_ref[...], ge_ref[...], be_ref[...])
    y_ret = jnp.dot(yr_ref[...], wro_ref[...], preferred_element_type=F32)
    y_fox = jnp.dot(yf_ref[...], wfo_ref[...], preferred_element_type=F32)
    merged = (jax.nn.sigmoid(gr_ref[...].astype(F32) + bg_ref[0:1, :]) * y_ret
              + jax.nn.sigmoid(gf_ref[...].astype(F32) + bg_ref[1:2, :]) * y_fox)
    out = jnp.dot(merged.astype(BF16), wout_ref[...], preferred_element_type=F32)
    h1 = _layer_norm(alpha * h0 + out, g1_ref[...], b1_ref[...])
    h1b = h1.astype(BF16)
    gate = jax.nn.sigmoid(jnp.dot(h1b, wpg_ref[...], preferred_element_type=F32) + bpg_ref[...])
    ple = gate * jnp.dot(p_ref[...].astype(BF16), wple_ref[...], preferred_element_type=F32)
    base_ref[...] = alpha * h1 + ple
    ht_ref[...] = h1.T.astype(BF16)


def _mix(x2, yr, yf, proj, p2, ge, be, bg, wro, wfo, wout, g1, b1, wpg, bpg, wple, alpha, tm=512):
    t = x2.shape[0]
    full = lambda shape: pl.BlockSpec(shape, lambda i: (0,) * len(shape))
    return pl.pallas_call(
        functools.partial(_mix_kernel, alpha=alpha),
        grid=(t // tm,),
        in_specs=[
            pl.BlockSpec((tm, D_MODEL), lambda i: (i, 0)),
            pl.BlockSpec((tm, RET_V_W), lambda i: (i, 0)),
            pl.BlockSpec((tm, FOX_W), lambda i: (i, 0)),
            pl.BlockSpec((tm, D_MODEL), lambda i: (i, OFF_GR // D_MODEL)),
            pl.BlockSpec((tm, D_MODEL), lambda i: (i, OFF_GF // D_MODEL)),
            pl.BlockSpec((tm, PLE_DIM), lambda i: (i, 0)),
            full((1, D_MODEL)), full((1, D_MODEL)), full((2, D_MODEL)),
            full((RET_V_W, D_MODEL)), full((FOX_W, D_MODEL)), full((D_MODEL, D_MODEL)),
            full((1, D_MODEL)), full((1, D_MODEL)),
            full((D_MODEL, D_MODEL)), full((1, D_MODEL)), full((PLE_DIM, D_MODEL)),
        ],
        out_specs=[
            pl.BlockSpec((tm, D_MODEL), lambda i: (i, 0)),
            pl.BlockSpec((D_MODEL, tm), lambda i: (0, i)),
        ],
        out_shape=[jax.ShapeDtypeStruct((t, D_MODEL), F32), jax.ShapeDtypeStruct((D_MODEL, t), BF16)],
        compiler_params=_params("parallel"),
        name="mix",
    )(x2, yr, yf, proj, proj, p2, ge, be, bg, wro, wfo, wout, g1, b1, wpg, bpg, wple)


def _top16(sc):
    nk = sc.shape[0]
    iota = lax.broadcasted_iota(jnp.int32, sc.shape, 0).astype(F32)
    rank = jnp.full(sc.shape, float(PEER_TOPK), F32)
    vals = []
    for a in range(PEER_TOPK):
        m = jnp.max(sc, axis=0, keepdims=True)
        first = jnp.min(jnp.where(sc == m, iota, float(nk)), axis=0, keepdims=True)
        hit = iota == first
        sc = jnp.where(hit, NEG_BIG, sc)
        rank = jnp.where(hit, float(a), rank)
        vals.append(m)
    return vals, rank


def _mark(a):
    return MARK_BASE * (1.0 + (a + 1) / 64.0)


def _top16_no_ties(chains):
    vals = [[] for _ in chains]
    for a in range(PEER_TOPK):
        ms = [jnp.max(sc, axis=0, keepdims=True) for sc in chains]
        chains = [jnp.where(sc == m, _mark(a), sc) for sc, m in zip(chains, ms)]
        for v, m in zip(vals, ms):
            v.append(m)
    return vals, chains


def _route_kernel(ht_ref, wq_ref, keys_ref, ct_ref, w0_ref, r1_ref, w1_ref,
                  q_scr, sc_scr, rank_scr, val_scr, cnt_scr, z_scr, *, lanes):
    tb = ht_ref.shape[1]
    groups = [slice(g * lanes, (g + 1) * lanes) for g in range(tb // lanes)]
    q_scr[...] = jnp.dot(wq_ref[...], ht_ref[...], preferred_element_type=F32).astype(BF16)
    val_scr[...] = jnp.zeros_like(val_scr)

    for hc in range(2 * PEER_HEADS):
        sc_scr[hc] = jnp.dot(keys_ref[hc], q_scr[hc * PEER_HALF:(hc + 1) * PEER_HALF, :],
                             preferred_element_type=F32)

    def store_vals(c, h, sl, vals):
        own_row = lax.broadcasted_iota(jnp.int32, (PEER_HEADS, lanes), 0) == h
        for a in range(PEER_TOPK):
            val_scr[c, a, :, sl] = jnp.where(own_row, vals[a], val_scr[c, a, :, sl])

    def select_cells(exact):
        v0 = [val_scr[0, a] for a in range(PEER_TOPK)]
        v1 = [val_scr[1, b] for b in range(PEER_TOPK)]
        cand = [v0[a] + v1[b] for a, b in CAND_CELLS]
        flat = [float(a * PEER_TOPK + b) for a, b in CAND_CELLS]
        sel = [jnp.zeros_like(v0[0]) for _ in CAND_CELLS]
        for _ in range(PEER_TOPK):
            m = functools.reduce(jnp.maximum, cand)
            if exact:
                first = functools.reduce(jnp.minimum,
                                         [jnp.where(cv == m, fi, 1e9) for cv, fi in zip(cand, flat)])
                hits = [first == fi for fi in flat]
            else:
                hits = [cv == m for cv in cand]
            cand = [jnp.where(hit, NEG_BIG, cv) for hit, cv in zip(hits, cand)]
            sel = [jnp.where(hit, 1.0, sv) for hit, sv in zip(hits, sel)]
        e0 = [jnp.exp(v0[a] - v0[0]) for a in range(PEER_TOPK)]
        e1 = [jnp.exp(v1[b] - v1[0]) for b in range(PEER_TOPK)]
        z = jnp.zeros_like(v0[0])
        cnt = [jnp.zeros_like(v0[0]) for _ in range(PEER_TOPK)]
        for t, (a, b) in enumerate(CAND_CELLS):
            z = z + sel[t] * (e0[a] * e1[b])
            cnt[a] = cnt[a] + sel[t]
        z_scr[...] = 1.0 / z
        for a in range(PEER_TOPK):
            cnt_scr[a] = cnt[a]
        return functools.reduce(jnp.add, sel)

    def fast_head(h, bad):
        for sl in groups:
            vals, marked = _top16_no_ties([sc_scr[2 * h, :, sl], sc_scr[2 * h + 1, :, sl]])
            for c in range(2):
                taken = marked[c] <= MARK_BASE
                unit = marked[c] * -(2.0 ** -100) * (2.0 ** -27)
                rank_scr[2 * h + c, :, sl] = jnp.where(taken, (unit - 1.0) * 64.0 - 1.0, float(PEER_TOPK))
                n_taken = jnp.sum(jnp.where(taken, 1.0, 0.0), axis=0, keepdims=True)
                bad = jnp.maximum(bad, jnp.abs(n_taken - float(PEER_TOPK)))
                store_vals(c, h, sl, vals[c])
        return bad

    bad = jnp.max(lax.fori_loop(0, PEER_HEADS, fast_head, jnp.zeros((1, lanes), F32)))
    bad = jnp.maximum(bad, jnp.max(jnp.abs(select_cells(False) - float(PEER_TOPK))))

    @pl.when(bad > 0.0)
    def _():
        def exact_vec(hc, carry):
            for sl in groups:
                vals, rank = _top16(sc_scr[hc, :, sl])
                rank_scr[hc, :, sl] = rank
                store_vals(hc % 2, hc // 2, sl, vals)
            return carry

        lax.fori_loop(0, 2 * PEER_HEADS, exact_vec, 0)
        select_cells(True)

    for h in range(PEER_HEADS):
        s0 = sc_scr[2 * h]
        s1 = sc_scr[2 * h + 1]
        rank0 = rank_scr[2 * h]
        ct = jnp.zeros_like(s0)
        for a in range(PEER_TOPK):
            ct = jnp.where(rank0 == float(a), cnt_scr[a, h:h + 1, :], ct)
        ct_ref[h] = ct
        w0_ref[h] = jnp.exp(s0 - val_scr[0, 0, h:h + 1, :]) * z_scr[h:h + 1, :]
        w1_ref[h] = jnp.exp(s1 - val_scr[1, 0, h:h + 1, :]).astype(BF16)
        r1_ref[h] = rank_scr[2 * h + 1].astype(BF16)


def _route(ht, wq_t, keys, tb=512, lanes=128):
    t = ht.shape[1]
    nq = 2 * PEER_HEADS * PEER_HALF
    routed = lambda dt: jax.ShapeDtypeStruct((PEER_HEADS, PEER_NKEYS, t), dt)
    rspec = pl.BlockSpec((PEER_HEADS, PEER_NKEYS, tb), lambda n: (0, 0, n))
    return pl.pallas_call(
        functools.partial(_route_kernel, lanes=lanes),
        grid=(t // tb,),
        in_specs=[
            pl.BlockSpec((D_MODEL, tb), lambda n: (0, n)),
            pl.BlockSpec((nq, D_MODEL), lambda n: (0, 0)),
            pl.BlockSpec((2 * PEER_HEADS, PEER_NKEYS, PEER_HALF), lambda n: (0, 0, 0)),
        ],
        out_specs=[rspec, rspec, rspec, rspec],
        out_shape=[routed(F32), routed(F32), routed(BF16), routed(BF16)],
        scratch_shapes=[
            pltpu.VMEM((nq, tb), BF16),
            pltpu.VMEM((2 * PEER_HEADS, PEER_NKEYS, tb), F32),
            pltpu.VMEM((2 * PEER_HEADS, PEER_NKEYS, tb), F32),
            pltpu.VMEM((2, PEER_TOPK, PEER_HEADS, tb), F32),
            pltpu.VMEM((PEER_TOPK, PEER_HEADS, tb), F32),
            pltpu.VMEM((PEER_HEADS, tb), F32),
        ],
        compiler_params=_params("parallel"),
        name="route",
    )(ht, wq_t, keys)


def _peer_kernel(ht_ref, u_ref, vt_ref, ct_ref, w0_ref, r1_ref, w1_ref, base_ref, g2_ref, b2_ref,
                 o_ref, hid_scr, a_scr, acc_scr, r1_scr, w1_scr, *, eb, lanes):
    e = pl.program_id(1)
    tb = ht_ref.shape[1]
    sub = 16

    @pl.when(e == 0)
    def _():
        acc_scr[...] = jnp.zeros_like(acc_scr)
        r1_scr[...] = r1_ref[...]
        w1_scr[...] = w1_ref[...]

    hid_scr[...] = jnp.dot(u_ref[...], ht_ref[...], preferred_element_type=F32)
    for il in range(eb // PEER_NKEYS):
        for lg in range(tb // lanes):
            sl = slice(lg * lanes, (lg + 1) * lanes)
            cts = [jnp.broadcast_to(ct_ref[h, il:il + 1, sl], (sub, lanes)).astype(BF16)
                   for h in range(PEER_HEADS)]
            w0s = [jnp.broadcast_to(0.5 * w0_ref[h, il:il + 1, sl], (sub, lanes)).astype(BF16)
                   for h in range(PEER_HEADS)]
            for jb in range(PEER_NKEYS // sub):
                js = slice(jb * sub, (jb + 1) * sub)
                gsum = None
                for h in range(PEER_HEADS):
                    w1 = w1_scr[h, js, sl]
                    term = jnp.where(r1_scr[h, js, sl] < cts[h], w1, jnp.zeros_like(w1)) * w0s[h]
                    gsum = term if gsum is None else gsum + term
                rows = slice(il * PEER_NKEYS + jb * sub, il * PEER_NKEYS + (jb + 1) * sub)
                hv = hid_scr[rows, sl].astype(BF16)
                act = hv * (1.0 + lax.erf(hv * jnp.asarray(1.0 / math.sqrt(2.0), BF16)))
                a_scr[rows, sl] = act * gsum
    acc_scr[...] += jnp.dot(vt_ref[...], a_scr[...], preferred_element_type=F32)

    @pl.when(e == pl.num_programs(1) - 1)
    def _():
        o_ref[...] = _layer_norm(base_ref[...] + acc_scr[...].T, g2_ref[...], b2_ref[...])


def _peer(ht, u_bf, vt_bf, ct, w0, r1, w1, base, g2, b2, tb=512, eb=2048, lanes=128):
    t = ht.shape[1]
    kr = eb // PEER_NKEYS
    rspec = pl.BlockSpec((PEER_HEADS, PEER_NKEYS, tb), lambda n, e: (0, 0, n))
    kspec = pl.BlockSpec((PEER_HEADS, kr, tb), lambda n, e: (0, e, n))
    return pl.pallas_call(
        functools.partial(_peer_kernel, eb=eb, lanes=lanes),
        grid=(t // tb, PEER_NEXP // eb),
        in_specs=[
            pl.BlockSpec((D_MODEL, tb), lambda n, e: (0, n)),
            pl.BlockSpec((eb, D_MODEL), lambda n, e: (e, 0)),
            pl.BlockSpec((D_MODEL, eb), lambda n, e: (0, e)),
            kspec, kspec, rspec, rspec,
            pl.BlockSpec((tb, D_MODEL), lambda n, e: (n, 0)),
            pl.BlockSpec((1, D_MODEL), lambda n, e: (0, 0)),
            pl.BlockSpec((1, D_MODEL), lambda n, e: (0, 0)),
        ],
        out_specs=pl.BlockSpec((tb, D_MODEL), lambda n, e: (n, 0)),
        out_shape=jax.ShapeDtypeStruct((t, D_MODEL), F32),
        scratch_shapes=[
            pltpu.VMEM((eb, tb), F32),
            pltpu.VMEM((eb, tb), BF16),
            pltpu.VMEM((D_MODEL, tb), F32),
            pltpu.VMEM((PEER_HEADS, PEER_NKEYS, tb), BF16),
            pltpu.VMEM((PEER_HEADS, PEER_NKEYS, tb), BF16),
        ],
        compiler_params=_params("parallel", "arbitrary"),
        name="peer",
    )(ht, u_bf, vt_bf, ct, w0, r1, w1, base, g2, b2)


def _retention_tables(seq):
    half = RET_DK // 2
    inv = ROPE_BASE ** (-jnp.arange(half, dtype=F32) / half)
    ang = jnp.arange(seq, dtype=jnp.int32).astype(F32)[:, None] * inv[None, :]
    cos, sin = jnp.cos(ang), jnp.sin(ang)
    cos_t = jnp.concatenate([cos, cos], axis=-1)
    sin_t = jnp.concatenate([-sin, sin], axis=-1)
    c = RET_CHUNK
    log_g = jnp.log(1.0 - 2.0 ** (-5.0 - jnp.arange(RET_HEADS, dtype=F32)))
    idx = jnp.arange(c, dtype=F32)
    diff = idx[:, None] - idx[None, :]
    intra = jnp.where(diff >= 0, jnp.exp(log_g[:, None, None] * jnp.maximum(diff, 0.0)), 0.0)
    qdec = jnp.broadcast_to(jnp.exp(log_g[:, None] * (idx + 1.0))[..., None], (RET_HEADS, c, RET_DK))
    kdec = jnp.broadcast_to(jnp.exp(log_g[:, None] * (c - 1.0 - idx))[..., None], (RET_HEADS, c, RET_DK))
    chunk_decay = tuple(math.exp(math.log(1.0 - 2.0 ** (-5.0 - h)) * c) for h in range(RET_HEADS))
    return cos_t, sin_t, intra, qdec, kdec, chunk_decay


def _pack_w_in(w):
    rq, rk, rv, rg, fq, fk, fv, ff, gr, gf = jnp.split(
        w, [512, 1024, 2048, 3072, 3584, 4096, 4608, 4616, 5640], axis=-1)
    main = jnp.concatenate([rq, rk, rv, rg, gr, gf, fq, fk, fv], axis=-1).astype(BF16)
    ff_pad = jnp.zeros((w.shape[0], FOX_HEADS // 2, 128), w.dtype).at[:, :, 0:2].set(
        ff.reshape(w.shape[0], FOX_HEADS // 2, 2)).reshape(w.shape[0], FF_W).astype(BF16)
    return main, ff_pad


def kernel(x, p, ln_emb_g, ln_emb_b, w_in, b_forget, b_branch_gate, w_ret_o, w_fox_o, w_out, ln1_g, ln1_b,
           w_peer_q, peer_sub_keys, peer_u, peer_v, w_ple_gate, b_ple_gate, w_ple, ln2_g, ln2_b):
    batch, seq, d = x.shape
    depth = w_in.shape[0]
    assert depth == 1 and d == D_MODEL, "the trunk-entry norm is fused into the single layer"
    t = batch * seq
    alpha = (2.0 * depth) ** 0.25
    fox_tq = min(512, seq)
    row = lambda v: v.reshape(1, -1).astype(F32)
    cos_t, sin_t, intra, qdec, kdec, chunk_decay = _retention_tables(seq)

    x2 = x.reshape(t, d)
    ge, be = row(ln_emb_g), row(ln_emb_b)
    w_main, w_ff = _pack_w_in(w_in[0])
    bf_pad = jnp.zeros((FOX_HEADS // 2, 128), F32).at[:, 0:2].set(
        b_forget[0].astype(F32).reshape(FOX_HEADS // 2, 2)).reshape(1, FF_W)
    proj, ff = _in_proj(x2, ge, be, w_main, w_ff)
    yr = _retention(proj, cos_t, sin_t, intra, qdec, kdec, chunk_decay, batch, seq)
    c, ct = _fcumsum(ff, bf_pad, batch, seq, fox_tq)
    yf = _fox(proj, c, ct, batch, seq, fox_tq)
    base, ht = _mix(x2, yr, yf, proj, p[0].reshape(t, PLE_DIM), ge, be, b_branch_gate[0].astype(F32),
                    w_ret_o[0].astype(BF16), w_fox_o[0].astype(BF16), w_out[0].astype(BF16),
                    row(ln1_g[0]), row(ln1_b[0]), w_ple_gate[0].astype(BF16), row(b_ple_gate[0]),
                    w_ple[0].astype(BF16), alpha)
    keys = peer_sub_keys[0].reshape(2 * PEER_HEADS, PEER_NKEYS, PEER_HALF).astype(BF16)
    ct_r, w0, r1, w1 = _route(ht, w_peer_q[0].T.astype(BF16), keys)
    out = _peer(ht, peer_u[0].astype(BF16), peer_v[0].T.astype(BF16), ct_r, w0, r1, w1, base,
                row(ln2_g[0]), row(ln2_b[0]))
    return out.reshape(batch, seq, d)
```

```python
import functools
import math

import jax
import jax.numpy as jnp
from jax import lax
from jax.experimental import pallas as pl
from jax.experimental.pallas import tpu as pltpu

F32 = jnp.float32
BF16 = jnp.bfloat16

D_MODEL = 1024
RET_HEADS = 4
RET_DK = 128
RET_DV = 256
RET_CHUNK = 128
FOX_HEADS = 8
FOX_DH = 64
PEER_HEADS = 8
PEER_NKEYS = 128
PEER_NEXP = PEER_NKEYS * PEER_NKEYS
PEER_HALF = 128
PEER_TOPK = 16
PLE_DIM = 256
LN_EPS = 1e-5
ROPE_BASE = 10000.0

RET_QK_W = RET_HEADS * RET_DK
RET_V_W = RET_HEADS * RET_DV
FOX_W = FOX_HEADS * FOX_DH

OFF_RQ, OFF_RK, OFF_RV, OFF_RG = 0, 512, 1024, 2048
OFF_GR, OFF_GF = 3072, 4096
OFF_FQ, OFF_FK, OFF_FV = 5120, 5632, 6144
PROJ_W = 6656
FF_W = 512

V7X_VMEM_LIMIT = 56 * 1024 * 1024
NEG_BIG = -3.0e38
MARK_BASE = -(2.0 ** 127)
CAND_CELLS = tuple((a, b) for a in range(PEER_TOPK) for b in range(PEER_TOPK)
                   if (a + 1) * (b + 1) <= PEER_TOPK)


def _layer_norm(x, g, b):
    mu = jnp.mean(x, axis=-1, keepdims=True)
    xc = x - mu
    var = jnp.mean(xc * xc, axis=-1, keepdims=True)
    return xc * lax.rsqrt(var + LN_EPS) * g + b


def _params(*sem):
    return pltpu.CompilerParams(dimension_semantics=sem, vmem_limit_bytes=V7X_VMEM_LIMIT)


def _inproj_kernel(x_ref, g_ref, b_ref, w_ref, wff_ref, o_ref, ff_ref, h_scr):
    @pl.when(pl.program_id(1) == 0)
    def _():
        hb = _layer_norm(x_ref[...], g_ref[...], b_ref[...]).astype(BF16)
        h_scr[...] = hb
        ff_ref[...] = jnp.dot(hb, wff_ref[...], preferred_element_type=F32)

    o_ref[...] = jnp.dot(h_scr[...], w_ref[...], preferred_element_type=F32).astype(BF16)


def _in_proj(x2, g, b, w_main, w_ff, tm=1024, tn=1664):
    t = x2.shape[0]
    return pl.pallas_call(
        _inproj_kernel,
        grid=(t // tm, PROJ_W // tn),
        in_specs=[
            pl.BlockSpec((tm, D_MODEL), lambda i, j: (i, 0)),
            pl.BlockSpec((1, D_MODEL), lambda i, j: (0, 0)),
            pl.BlockSpec((1, D_MODEL), lambda i, j: (0, 0)),
            pl.BlockSpec((D_MODEL, tn), lambda i, j: (0, j)),
            pl.BlockSpec((D_MODEL, FF_W), lambda i, j: (0, 0)),
        ],
        out_specs=[
            pl.BlockSpec((tm, tn), lambda i, j: (i, j)),
            pl.BlockSpec((tm, FF_W), lambda i, j: (i, 0)),
        ],
        out_shape=[jax.ShapeDtypeStruct((t, PROJ_W), BF16), jax.ShapeDtypeStruct((t, FF_W), F32)],
        scratch_shapes=[pltpu.VMEM((tm, D_MODEL), BF16)],
        compiler_params=_params("parallel", "arbitrary"),
        name="in_proj",
    )(x2, g, b, w_main, w_ff)


def _retention_kernel(q_ref, k_ref, v_ref, g_ref, cos_ref, sin_ref, intra_ref, qd_ref, kd_ref,
                      o_ref, r_scr, *, chunk_decay):
    @pl.when(pl.program_id(1) == 0)
    def _():
        r_scr[...] = jnp.zeros_like(r_scr)

    cos = cos_ref[...]
    sin = sin_ref[...]
    for h in range(RET_HEADS):
        q = q_ref[:, h * RET_DK:(h + 1) * RET_DK].astype(F32)
        k = k_ref[:, h * RET_DK:(h + 1) * RET_DK].astype(F32)
        v = v_ref[:, h * RET_DV:(h + 1) * RET_DV]
        qr = q * cos + pltpu.roll(q, RET_DK // 2, 1) * sin
        kr = (k * cos + pltpu.roll(k, RET_DK // 2, 1) * sin) * (RET_DK ** -0.5)
        s = lax.dot_general(qr.astype(BF16), kr.astype(BF16), (((1,), (1,)), ((), ())),
                            preferred_element_type=F32) * intra_ref[h]
        inner = jnp.dot(s.astype(BF16), v, preferred_element_type=F32)
        r_old = r_scr[h]
        cross = jnp.dot((qr * qd_ref[h]).astype(BF16), r_old.astype(BF16), preferred_element_type=F32)
        kv = lax.dot_general((kr * kd_ref[h]).astype(BF16), v, (((0,), (0,)), ((), ())),
                             preferred_element_type=F32)
        r_scr[h] = chunk_decay[h] * r_old + kv
        y = inner + cross
        mu = jnp.mean(y, axis=-1, keepdims=True)
        yc = y - mu
        var = jnp.mean(yc * yc, axis=-1, keepdims=True)
        yn = yc * lax.rsqrt(var + LN_EPS)
        gate = g_ref[:, h * RET_DV:(h + 1) * RET_DV].astype(F32)
        o_ref[:, h * RET_DV:(h + 1) * RET_DV] = (gate * jax.nn.sigmoid(gate) * yn).astype(BF16)


def _retention(proj, cos_t, sin_t, intra, qdec, kdec, chunk_decay, batch, seq):
    c = RET_CHUNK
    nc = seq // c
    row = lambda b, j: b * nc + j
    return pl.pallas_call(
        functools.partial(_retention_kernel, chunk_decay=chunk_decay),
        grid=(batch, nc),
        in_specs=[
            pl.BlockSpec((c, RET_QK_W), lambda b, j: (row(b, j), OFF_RQ // RET_QK_W)),
            pl.BlockSpec((c, RET_QK_W), lambda b, j: (row(b, j), OFF_RK // RET_QK_W)),
            pl.BlockSpec((c, RET_V_W), lambda b, j: (row(b, j), OFF_RV // RET_V_W)),
            pl.BlockSpec((c, RET_V_W), lambda b, j: (row(b, j), OFF_RG // RET_V_W)),
            pl.BlockSpec((c, RET_DK), lambda b, j: (j, 0)),
            pl.BlockSpec((c, RET_DK), lambda b, j: (j, 0)),
            pl.BlockSpec((RET_HEADS, c, c), lambda b, j: (0, 0, 0)),
            pl.BlockSpec((RET_HEADS, c, RET_DK), lambda b, j: (0, 0, 0)),
            pl.BlockSpec((RET_HEADS, c, RET_DK), lambda b, j: (0, 0, 0)),
        ],
        out_specs=pl.BlockSpec((c, RET_V_W), lambda b, j: (row(b, j), 0)),
        out_shape=jax.ShapeDtypeStruct((batch * seq, RET_V_W), BF16),
        scratch_shapes=[pltpu.VMEM((RET_HEADS, RET_DK, RET_DV), F32)],
        compiler_params=_params("parallel", "arbitrary"),
        name="retention",
    )(proj, proj, proj, proj, cos_t, sin_t, intra, qdec, kdec)


def _fcumsum_kernel(ff_ref, bf_ref, tri_ref, c_ref, ct_ref, *, blk):
    seq = ff_ref.shape[0]
    carry = jnp.zeros((1, FF_W), F32)
    for i in range(seq // blk):
        lf = jax.nn.log_sigmoid(ff_ref[i * blk:(i + 1) * blk, :] + bf_ref[...])
        cs = jnp.dot(tri_ref[...], lf, preferred_element_type=F32,
                     precision=lax.Precision.HIGHEST) + carry
        c_ref[i * blk:(i + 1) * blk, :] = cs
        carry = cs[blk - 1:blk, :]
        for p in range(FOX_HEADS // 2):
            ct_ref[0, p, i] = cs[:, p * 128:(p + 1) * 128].T[0:8, :]


def _fcumsum(ff, bf_pad, batch, seq, blk):
    tri = (lax.broadcasted_iota(jnp.int32, (blk, blk), 0)
           >= lax.broadcasted_iota(jnp.int32, (blk, blk), 1)).astype(F32)
    nb = seq // blk
    return pl.pallas_call(
        functools.partial(_fcumsum_kernel, blk=blk),
        grid=(batch,),
        in_specs=[
            pl.BlockSpec((seq, FF_W), lambda b: (b, 0)),
            pl.BlockSpec((1, FF_W), lambda b: (0, 0)),
            pl.BlockSpec((blk, blk), lambda b: (0, 0)),
        ],
        out_specs=[
            pl.BlockSpec((seq, FF_W), lambda b: (b, 0)),
            pl.BlockSpec((1, FOX_HEADS // 2, nb, 8, blk), lambda b: (b, 0, 0, 0, 0)),
        ],
        out_shape=[jax.ShapeDtypeStruct((batch * seq, FF_W), F32),
                   jax.ShapeDtypeStruct((batch, FOX_HEADS // 2, nb, 8, blk), F32)],
        compiler_params=_params("parallel"),
        name="fcumsum",
    )(ff, bf_pad, tri)


def _fox_kernel(q_ref, k_ref, v_ref, c_ref, ct_ref, o_ref, vt_scr, ckb_scr, m_scr, l_scr, acc_scr, *, tq):
    qi = pl.program_id(2)
    seq = k_ref.shape[0]
    nkv = seq // tq

    @pl.when(qi == 0)
    def _():
        for j in range(nkv):
            vt_scr[j] = v_ref[j * tq:(j + 1) * tq, :].astype(F32).T.astype(BF16)
        for r in range(2):
            ckb_scr[r] = jnp.broadcast_to(c_ref[:, r:r + 1], (seq, 128))

    lane = lax.broadcasted_iota(jnp.int32, (tq, 128), 1)
    key_i = lax.broadcasted_iota(jnp.int32, (tq, tq), 0)
    qry_i = lax.broadcasted_iota(jnp.int32, (tq, tq), 1)
    q = q_ref[...] * jnp.asarray(FOX_DH ** -0.5, BF16)
    qh = [jnp.where((lane >= r * FOX_DH) & (lane < (r + 1) * FOX_DH), q, jnp.zeros_like(q)) for r in range(2)]
    cq = [ct_ref[0, 0, qi, r:r + 1, :] for r in range(2)]
    m_scr[...] = jnp.full_like(m_scr, NEG_BIG)
    l_scr[...] = jnp.zeros_like(l_scr)
    acc_scr[...] = jnp.zeros_like(acc_scr)

    def block(j, diag):
        start = pl.multiple_of(j * tq, tq)
        kb = k_ref[pl.ds(start, tq), :]
        vtb = vt_scr[j]
        s = [lax.dot_general(kb, qh[r], (((1,), (1,)), ((), ())), preferred_element_type=F32)
             for r in range(2)]
        ck = [ckb_scr[r, pl.ds(start, tq), :] for r in range(2)]
        s = [s[r] + cq[r] - jnp.concatenate([ck[r]] * (tq // 128), axis=1) for r in range(2)]
        if diag:
            s = [jnp.where(key_i <= qry_i, s[r], -1e30) for r in range(2)]
        m_old = [m_scr[r] for r in range(2)]
        m_new = [jnp.maximum(m_old[r], jnp.max(s[r], axis=0, keepdims=True)) for r in range(2)]
        alpha = [jnp.exp(m_old[r] - m_new[r]) for r in range(2)]
        p = [jnp.exp(s[r] - m_new[r]) for r in range(2)]
        for r in range(2):
            l_scr[r] = alpha[r] * l_scr[r] + jnp.sum(p[r], axis=0, keepdims=True)
            pv = jnp.dot(vtb[r * FOX_DH:(r + 1) * FOX_DH, :], p[r].astype(BF16), preferred_element_type=F32)
            acc_scr[r] = alpha[r] * acc_scr[r] + pv
            m_scr[r] = m_new[r]

    def body(j, carry):
        block(j, False)
        return carry

    lax.fori_loop(0, qi, body, 0)
    block(qi, True)
    out_t = jnp.concatenate([acc_scr[r] / l_scr[r] for r in range(2)], axis=0)
    o_ref[...] = out_t.T.astype(BF16)


def _fox(proj, c, ct, batch, seq, tq):
    nq = seq // tq
    np_ = FOX_HEADS // 2
    return pl.pallas_call(
        functools.partial(_fox_kernel, tq=tq),
        grid=(batch, np_, nq),
        in_specs=[
            pl.BlockSpec((tq, 128), lambda b, p, i: (b * nq + i, OFF_FQ // 128 + p)),
            pl.BlockSpec((seq, 128), lambda b, p, i: (b, OFF_FK // 128 + p)),
            pl.BlockSpec((seq, 128), lambda b, p, i: (b, OFF_FV // 128 + p)),
            pl.BlockSpec((seq, 128), lambda b, p, i: (b, p)),
            pl.BlockSpec((1, 1, nq, 8, tq), lambda b, p, i: (b, p, 0, 0, 0)),
        ],
        out_specs=pl.BlockSpec((tq, 128), lambda b, p, i: (b * nq + i, p)),
        out_shape=jax.ShapeDtypeStruct((batch * seq, FOX_W), BF16),
        scratch_shapes=[
            pltpu.VMEM((nq, 128, tq), BF16),
            pltpu.VMEM((2, seq, 128), F32),
            pltpu.VMEM((2, 1, tq), F32),
            pltpu.VMEM((2, 1, tq), F32),
            pltpu.VMEM((2, FOX_DH, tq), F32),
        ],
        compiler_params=_params("parallel", "parallel", "arbitrary"),
        name="fox",
    )(proj, proj, proj, c, ct)


def _mix_kernel(x_ref, yr_ref, yf_ref, gr_ref, gf_ref, p_ref, ge_ref, be_ref, bg_ref, wro_ref, wfo_ref,
                wout_ref, g1_ref, b1_ref, wpg_ref, bpg_ref, wple_ref, base_ref, ht_ref, *, alpha):
    h0 = _layer_norm(x_ref[...], ge_ref[...], be_ref[...])
    y_ret = jnp.dot(yr_ref[...], wro_ref[...], preferred_element_type=F32)
    y_fox = jnp.dot(yf_ref[...], wfo_ref[...], preferred_element_type=F32)
    merged = (jax.nn.sigmoid(gr_ref[...].astype(F32) + bg_ref[0:1, :]) * y_ret
              + jax.nn.sigmoid(gf_ref[...].astype(F32) + bg_ref[1:2, :]) * y_fox)
    out = jnp.dot(merged.astype(BF16), wout_ref[...], preferred_element_type=F32)
    h1 = _layer_norm(alpha * h0 + out, g1_ref[...], b1_ref[...])
    h1b = h1.astype(BF16)
    gate = jax.nn.sigmoid(jnp.dot(h1b, wpg_ref[...], preferred_element_type=F32) + bpg_ref[...])
    ple = gate * jnp.dot(p_ref[...].astype(BF16), wple_ref[...], preferred_element_type=F32)
    base_ref[...] = alpha * h1 + ple
    ht_ref[...] = h1.T.astype(BF16)


def _mix(x2, yr, yf, proj, p2, ge, be, bg, wro, wfo, wout, g1, b1, wpg, bpg, wple, alpha, tm=512):
    t = x2.shape[0]
    full = lambda shape: pl.BlockSpec(shape, lambda i: (0,) * len(shape))
    return pl.pallas_call(
        functools.partial(_mix_kernel, alpha=alpha),
        grid=(t // tm,),
        in_specs=[
            pl.BlockSpec((tm, D_MODEL), lambda i: (i, 0)),
            pl.BlockSpec((tm, RET_V_W), lambda i: (i, 0)),
            pl.BlockSpec((tm, FOX_W), lambda i: (i, 0)),
            pl.BlockSpec((tm, D_MODEL), lambda i: (i, OFF_GR // D_MODEL)),
            pl.BlockSpec((tm, D_MODEL), lambda i: (i, OFF_GF // D_MODEL)),
            pl.BlockSpec((tm, PLE_DIM), lambda i: (i, 0)),
            full((1, D_MODEL)), full((1, D_MODEL)), full((2, D_MODEL)),
            full((RET_V_W, D_MODEL)), full((FOX_W, D_MODEL)), full((D_MODEL, D_MODEL)),
            full((1, D_MODEL)), full((1, D_MODEL)),
            full((D_MODEL, D_MODEL)), full((1, D_MODEL)), full((PLE_DIM, D_MODEL)),
        ],
        out_specs=[
            pl.BlockSpec((tm, D_MODEL), lambda i: (i, 0)),
            pl.BlockSpec((D_MODEL, tm), lambda i: (0, i)),
        ],
        out_shape=[jax.ShapeDtypeStruct((t, D_MODEL), F32), jax.ShapeDtypeStruct((D_MODEL, t), BF16)],
        compiler_params=_params("parallel"),
        name="mix",
    )(x2, yr, yf, proj, proj, p2, ge, be, bg, wro, wfo, wout, g1, b1, wpg, bpg, wple)


def _top16(sc):
    nk = sc.shape[0]
    iota = lax.broadcasted_iota(jnp.int32, sc.shape, 0).astype(F32)
    rank = jnp.full(sc.shape, float(PEER_TOPK), F32)
    vals = []
    for a in range(PEER_TOPK):
        m = jnp.max(sc, axis=0, keepdims=True)
        first = jnp.min(jnp.where(sc == m, iota, float(nk)), axis=0, keepdims=True)
        hit = iota == first
        sc = jnp.where(hit, NEG_BIG, sc)
        rank = jnp.where(hit, float(a), rank)
        vals.append(m)
    return vals, rank


def _mark(a):
    return MARK_BASE * (1.0 + (a + 1) / 64.0)


def _top16_no_ties(chains):
    vals = [[] for _ in chains]
    for a in range(PEER_TOPK):
        ms = [jnp.max(sc, axis=0, keepdims=True) for sc in chains]
        chains = [jnp.where(sc == m, _mark(a), sc) for sc, m in zip(chains, ms)]
        for v, m in zip(vals, ms):
            v.append(m)
    return vals, chains


def _route_kernel(ht_ref, wq_ref, keys_ref, ct_ref, w0_ref, r1_ref, w1_ref,
                  q_scr, sc_scr, rank_scr, val_scr, cnt_scr, z_scr, *, lanes):
    tb = ht_ref.shape[1]
    groups = [slice(g * lanes, (g + 1) * lanes) for g in range(tb // lanes)]
    q_scr[...] = jnp.dot(wq_ref[...], ht_ref[...], preferred_element_type=F32).astype(BF16)
    val_scr[...] = jnp.zeros_like(val_scr)

    for hc in range(2 * PEER_HEADS):
        sc_scr[hc] = jnp.dot(keys_ref[hc], q_scr[hc * PEER_HALF:(hc + 1) * PEER_HALF, :],
                             preferred_element_type=F32)

    def store_vals(c, h, sl, vals):
        own_row = lax.broadcasted_iota(jnp.int32, (PEER_HEADS, lanes), 0) == h
        for a in range(PEER_TOPK):
            val_scr[c, a, :, sl] = jnp.where(own_row, vals[a], val_scr[c, a, :, sl])

    def select_cells(exact):
        flat = [float(a * PEER_TOPK + b) for a, b in CAND_CELLS]
        excess = jnp.zeros((PEER_HEADS, lanes), F32)
        for sl in groups:
            v0 = [val_scr[0, a, :, sl] for a in range(PEER_TOPK)]
            v1 = [val_scr[1, b, :, sl] for b in range(PEER_TOPK)]
            cand = [v0[a] + v1[b] for a, b in CAND_CELLS]
            if exact:
                sel = [jnp.zeros_like(v0[0]) for _ in CAND_CELLS]
                for _ in range(PEER_TOPK):
                    m = functools.reduce(jnp.maximum, cand)
                    first = functools.reduce(
                        jnp.minimum, [jnp.where(cv == m, fi, 1e9) for cv, fi in zip(cand, flat)])
                    hits = [first == fi for fi in flat]
                    cand = [jnp.where(hit, NEG_BIG, cv) for hit, cv in zip(hits, cand)]
                    sel = [jnp.where(hit, 1.0, sv) for hit, sv in zip(hits, sel)]
            else:
                thr = functools.reduce(jnp.maximum, cand)
                for _ in range(PEER_TOPK - 1):
                    thr = functools.reduce(jnp.maximum, [jnp.where(cv < thr, cv, NEG_BIG) for cv in cand])
                sel = [jnp.where(cv >= thr, 1.0, 0.0) for cv in cand]
            e0 = [jnp.exp(v0[a] - v0[0]) for a in range(PEER_TOPK)]
            e1 = [jnp.exp(v1[b] - v1[0]) for b in range(PEER_TOPK)]
            z = jnp.zeros_like(v0[0])
            cnt = [jnp.zeros_like(v0[0]) for _ in range(PEER_TOPK)]
            for t, (a, b) in enumerate(CAND_CELLS):
                z = z + sel[t] * (e0[a] * e1[b])
                cnt[a] = cnt[a] + sel[t]
            z_scr[:, sl] = 1.0 / z
            for a in range(PEER_TOPK):
                cnt_scr[a, :, sl] = cnt[a]
            excess = jnp.maximum(excess, jnp.abs(functools.reduce(jnp.add, sel) - float(PEER_TOPK)))
        return excess

    def fast_head(h, bad):
        for sl in groups:
            vals, marked = _top16_no_ties([sc_scr[2 * h, :, sl], sc_scr[2 * h + 1, :, sl]])
            for c in range(2):
                taken = marked[c] <= MARK_BASE
                unit = marked[c] * -(2.0 ** -100) * (2.0 ** -27)
                rank_scr[2 * h + c, :, sl] = jnp.where(taken, (unit - 1.0) * 64.0 - 1.0, float(PEER_TOPK))
                n_taken = jnp.sum(jnp.where(taken, 1.0, 0.0), axis=0, keepdims=True)
                bad = jnp.maximum(bad, jnp.abs(n_taken - float(PEER_TOPK)))
                store_vals(c, h, sl, vals[c])
        return bad

    bad = jnp.max(lax.fori_loop(0, PEER_HEADS, fast_head, jnp.zeros((1, lanes), F32)))
    bad = jnp.maximum(bad, jnp.max(select_cells(False)))

    @pl.when(bad > 0.0)
    def _():
        def exact_vec(hc, carry):
            for sl in groups:
                vals, rank = _top16(sc_scr[hc, :, sl])
                rank_scr[hc, :, sl] = rank
                store_vals(hc % 2, hc // 2, sl, vals)
            return carry

        lax.fori_loop(0, 2 * PEER_HEADS, exact_vec, 0)
        select_cells(True)

    for h in range(PEER_HEADS):
        s0 = sc_scr[2 * h]
        s1 = sc_scr[2 * h + 1]
        rank0 = rank_scr[2 * h]
        ct = jnp.zeros_like(s0)
        for a in range(PEER_TOPK):
            ct = jnp.where(rank0 == float(a), cnt_scr[a, h:h + 1, :], ct)
        ct_ref[h] = ct
        w0_ref[h] = jnp.exp(s0 - val_scr[0, 0, h:h + 1, :]) * z_scr[h:h + 1, :]
        w1_ref[h] = jnp.exp(s1 - val_scr[1, 0, h:h + 1, :]).astype(BF16)
        r1_ref[h] = rank_scr[2 * h + 1].astype(BF16)


def _route(ht, wq_t, keys, tb=256, lanes=128):
    t = ht.shape[1]
    nq = 2 * PEER_HEADS * PEER_HALF
    routed = lambda dt: jax.ShapeDtypeStruct((PEER_HEADS, PEER_NKEYS, t), dt)
    rspec = pl.BlockSpec((PEER_HEADS, PEER_NKEYS, tb), lambda n: (0, 0, n))
    return pl.pallas_call(
        functools.partial(_route_kernel, lanes=lanes),
        grid=(t // tb,),
        in_specs=[
            pl.BlockSpec((D_MODEL, tb), lambda n: (0, n)),
            pl.BlockSpec((nq, D_MODEL), lambda n: (0, 0)),
            pl.BlockSpec((2 * PEER_HEADS, PEER_NKEYS, PEER_HALF), lambda n: (0, 0, 0)),
        ],
        out_specs=[rspec, rspec, rspec, rspec],
        out_shape=[routed(F32), routed(F32), routed(BF16), routed(BF16)],
        scratch_shapes=[
            pltpu.VMEM((nq, tb), BF16),
            pltpu.VMEM((2 * PEER_HEADS, PEER_NKEYS, tb), F32),
            pltpu.VMEM((2 * PEER_HEADS, PEER_NKEYS, tb), F32),
            pltpu.VMEM((2, PEER_TOPK, PEER_HEADS, tb), F32),
            pltpu.VMEM((PEER_TOPK, PEER_HEADS, tb), F32),
            pltpu.VMEM((PEER_HEADS, tb), F32),
        ],
        compiler_params=_params("parallel"),
        name="route",
    )(ht, wq_t, keys)


def _peer_kernel(ht_ref, u_ref, vt_ref, ct_ref, w0_ref, r1_ref, w1_ref, base_ref, g2_ref, b2_ref,
                 o_ref, hid_scr, a_scr, acc_scr, r1_scr, w1_scr, *, eb, lanes):
    e = pl.program_id(1)
    tb = ht_ref.shape[1]
    sub = 16

    @pl.when(e == 0)
    def _():
        acc_scr[...] = jnp.zeros_like(acc_scr)
        r1_scr[:, :, 0:tb] = r1_ref[...]
        w1_scr[:, :, lanes:tb + lanes] = w1_ref[...]

    hid_scr[...] = jnp.dot(u_ref[...], ht_ref[...], preferred_element_type=F32)
    for il in range(eb // PEER_NKEYS):
        for lg in range(tb // lanes):
            sl = slice(lg * lanes, (lg + 1) * lanes)
            sw = slice((lg + 1) * lanes, (lg + 2) * lanes)
            cts = [jnp.broadcast_to(ct_ref[h, il:il + 1, sl], (sub, lanes)).astype(BF16)
                   for h in range(PEER_HEADS)]
            w0s = [jnp.broadcast_to(0.5 * w0_ref[h, il:il + 1, sl], (sub, lanes)).astype(BF16)
                   for h in range(PEER_HEADS)]
            for jb in range(PEER_NKEYS // sub):
                js = slice(jb * sub, (jb + 1) * sub)
                gsum = None
                for h in range(PEER_HEADS):
                    gate0 = jnp.minimum(jnp.maximum(cts[h] - r1_scr[h, js, sl], 0.0), w0s[h])
                    term = gate0 * w1_scr[h, js, sw]
                    gsum = term if gsum is None else gsum + term
                rows = slice(il * PEER_NKEYS + jb * sub, il * PEER_NKEYS + (jb + 1) * sub)
                hv = hid_scr[rows, sl].astype(BF16)
                act = hv * (1.0 + lax.erf(hv * jnp.asarray(1.0 / math.sqrt(2.0), BF16)))
                a_scr[rows, sl] = act * gsum
    acc_scr[...] += jnp.dot(vt_ref[...], a_scr[...], preferred_element_type=F32)

    @pl.when(e == pl.num_programs(1) - 1)
    def _():
        o_ref[...] = _layer_norm(base_ref[...] + acc_scr[...].T, g2_ref[...], b2_ref[...])


def _peer(ht, u_bf, vt_bf, ct, w0, r1, w1, base, g2, b2, tb=512, eb=2048, lanes=128):
    t = ht.shape[1]
    kr = eb // PEER_NKEYS
    rspec = pl.BlockSpec((PEER_HEADS, PEER_NKEYS, tb), lambda n, e: (0, 0, n))
    kspec = pl.BlockSpec((PEER_HEADS, kr, tb), lambda n, e: (0, e, n))
    return pl.pallas_call(
        functools.partial(_peer_kernel, eb=eb, lanes=lanes),
        grid=(t // tb, PEER_NEXP // eb),
        in_specs=[
            pl.BlockSpec((D_MODEL, tb), lambda n, e: (0, n)),
            pl.BlockSpec((eb, D_MODEL), lambda n, e: (e, 0)),
            pl.BlockSpec((D_MODEL, eb), lambda n, e: (0, e)),
            kspec, kspec, rspec, rspec,
            pl.BlockSpec((tb, D_MODEL), lambda n, e: (n, 0)),
            pl.BlockSpec((1, D_MODEL), lambda n, e: (0, 0)),
            pl.BlockSpec((1, D_MODEL), lambda n, e: (0, 0)),
        ],
        out_specs=pl.BlockSpec((tb, D_MODEL), lambda n, e: (n, 0)),
        out_shape=jax.ShapeDtypeStruct((t, D_MODEL), F32),
        scratch_shapes=[
            pltpu.VMEM((eb, tb), F32),
            pltpu.VMEM((eb, tb), BF16),
            pltpu.VMEM((D_MODEL, tb), F32),
            pltpu.VMEM((PEER_HEADS, PEER_NKEYS, tb + lanes), BF16),
            pltpu.VMEM((PEER_HEADS, PEER_NKEYS, tb + lanes), BF16),
        ],
        compiler_params=_params("parallel", "arbitrary"),
        name="peer",
    )(ht, u_bf, vt_bf, ct, w0, r1, w1, base, g2, b2)


def _retention_tables(seq):
    half = RET_DK // 2
    inv = ROPE_BASE ** (-jnp.arange(half, dtype=F32) / half)
    ang = jnp.arange(seq, dtype=jnp.int32).astype(F32)[:, None] * inv[None, :]
    cos, sin = jnp.cos(ang), jnp.sin(ang)
    cos_t = jnp.concatenate([cos, cos], axis=-1)
    sin_t = jnp.concatenate([-sin, sin], axis=-1)
    c = RET_CHUNK
    log_g = jnp.log(1.0 - 2.0 ** (-5.0 - jnp.arange(RET_HEADS, dtype=F32)))
    idx = jnp.arange(c, dtype=F32)
    diff = idx[:, None] - idx[None, :]
    intra = jnp.where(diff >= 0, jnp.exp(log_g[:, None, None] * jnp.maximum(diff, 0.0)), 0.0)
    qdec = jnp.broadcast_to(jnp.exp(log_g[:, None] * (idx + 1.0))[..., None], (RET_HEADS, c, RET_DK))
    kdec = jnp.broadcast_to(jnp.exp(log_g[:, None] * (c - 1.0 - idx))[..., None], (RET_HEADS, c, RET_DK))
    chunk_decay = tuple(math.exp(math.log(1.0 - 2.0 ** (-5.0 - h)) * c) for h in range(RET_HEADS))
    return cos_t, sin_t, intra, qdec, kdec, chunk_decay


def _pack_w_in(w):
    rq, rk, rv, rg, fq, fk, fv, ff, gr, gf = jnp.split(
        w, [512, 1024, 2048, 3072, 3584, 4096, 4608, 4616, 5640], axis=-1)
    main = jnp.concatenate([rq, rk, rv, rg, gr, gf, fq, fk, fv], axis=-1).astype(BF16)
    ff_pad = jnp.zeros((w.shape[0], FOX_HEADS // 2, 128), w.dtype).at[:, :, 0:2].set(
        ff.reshape(w.shape[0], FOX_HEADS // 2, 2)).reshape(w.shape[0], FF_W).astype(BF16)
    return main, ff_pad


def kernel(x, p, ln_emb_g, ln_emb_b, w_in, b_forget, b_branch_gate, w_ret_o, w_fox_o, w_out, ln1_g, ln1_b,
           w_peer_q, peer_sub_keys, peer_u, peer_v, w_ple_gate, b_ple_gate, w_ple, ln2_g, ln2_b):
    batch, seq, d = x.shape
    depth = w_in.shape[0]
    assert depth == 1 and d == D_MODEL, "the trunk-entry norm is fused into the single layer"
    t = batch * seq
    alpha = (2.0 * depth) ** 0.25
    fox_tq = min(512, seq)
    row = lambda v: v.reshape(1, -1).astype(F32)
    cos_t, sin_t, intra, qdec, kdec, chunk_decay = _retention_tables(seq)

    x2 = x.reshape(t, d)
    ge, be = row(ln_emb_g), row(ln_emb_b)
    w_main, w_ff = _pack_w_in(w_in[0])
    bf_pad = jnp.zeros((FOX_HEADS // 2, 128), F32).at[:, 0:2].set(
        b_forget[0].astype(F32).reshape(FOX_HEADS // 2, 2)).reshape(1, FF_W)
    proj, ff = _in_proj(x2, ge, be, w_main, w_ff)
    yr = _retention(proj, cos_t, sin_t, intra, qdec, kdec, chunk_decay, batch, seq)
    c, ct = _fcumsum(ff, bf_pad, batch, seq, fox_tq)
    yf = _fox(proj, c, ct, batch, seq, fox_tq)
    base, ht = _mix(x2, yr, yf, proj, p[0].reshape(t, PLE_DIM), ge, be, b_branch_gate[0].astype(F32),
                    w_ret_o[0].astype(BF16), w_fox_o[0].astype(BF16), w_out[0].astype(BF16),
                    row(ln1_g[0]), row(ln1_b[0]), w_ple_gate[0].astype(BF16), row(b_ple_gate[0]),
                    w_ple[0].astype(BF16), alpha)
    keys = peer_sub_keys[0].reshape(2 * PEER_HEADS, PEER_NKEYS, PEER_HALF).astype(BF16)
    ct_r, w0, r1, w1 = _route(ht, w_peer_q[0].T.astype(BF16), keys)
    out = _peer(ht, peer_u[0].astype(BF16), peer_v[0].T.astype(BF16), ct_r, w0, r1, w1, base,
                row(ln2_g[0]), row(ln2_b[0]))
    return out.reshape(batch, seq, d)
```

```python
import functools
import math

import jax
import jax.numpy as jnp
from jax import lax
from jax.experimental import pallas as pl
from jax.experimental.pallas import tpu as pltpu

F32 = jnp.float32
BF16 = jnp.bfloat16

D_MODEL = 1024
RET_HEADS = 4
RET_DK = 128
RET_DV = 256
RET_CHUNK = 128
FOX_HEADS = 8
FOX_DH = 64
PEER_HEADS = 8
PEER_NKEYS = 128
PEER_NEXP = PEER_NKEYS * PEER_NKEYS
PEER_HALF = 128
PEER_TOPK = 16
PLE_DIM = 256
LN_EPS = 1e-5
ROPE_BASE = 10000.0

RET_QK_W = RET_HEADS * RET_DK
RET_V_W = RET_HEADS * RET_DV
FOX_W = FOX_HEADS * FOX_DH

OFF_RQ, OFF_RK, OFF_RV, OFF_RG = 0, 512, 1024, 2048
OFF_GR, OFF_GF = 3072, 4096
OFF_FQ, OFF_FK, OFF_FV = 5120, 5632, 6144
PROJ_W = 6656
FF_W = 512

V7X_VMEM_LIMIT = 56 * 1024 * 1024
NEG_BIG = -3.0e38
MARK_BASE = -(2.0 ** 127)
CAND_CELLS = tuple((a, b) for a in range(PEER_TOPK) for b in range(PEER_TOPK)
                   if (a + 1) * (b + 1) <= PEER_TOPK)


def _layer_norm(x, g, b):
    mu = jnp.mean(x, axis=-1, keepdims=True)
    xc = x - mu
    var = jnp.mean(xc * xc, axis=-1, keepdims=True)
    return xc * lax.rsqrt(var + LN_EPS) * g + b


def _params(*sem):
    return pltpu.CompilerParams(dimension_semantics=sem, vmem_limit_bytes=V7X_VMEM_LIMIT)


def _inproj_kernel(x_ref, g_ref, b_ref, w_ref, wff_ref, o_ref, ff_ref, h_scr):
    @pl.when(pl.program_id(1) == 0)
    def _():
        hb = _layer_norm(x_ref[...], g_ref[...], b_ref[...]).astype(BF16)
        h_scr[...] = hb
        ff_ref[...] = jnp.dot(hb, wff_ref[...], preferred_element_type=F32)

    o_ref[...] = jnp.dot(h_scr[...], w_ref[...], preferred_element_type=F32).astype(BF16)


def _in_proj(x2, g, b, w_main, w_ff, tm=1024, tn=3328):
    t = x2.shape[0]
    return pl.pallas_call(
        _inproj_kernel,
        grid=(t // tm, PROJ_W // tn),
        in_specs=[
            pl.BlockSpec((tm, D_MODEL), lambda i, j: (i, 0)),
            pl.BlockSpec((1, D_MODEL), lambda i, j: (0, 0)),
            pl.BlockSpec((1, D_MODEL), lambda i, j: (0, 0)),
            pl.BlockSpec((D_MODEL, tn), lambda i, j: (0, j)),
            pl.BlockSpec((D_MODEL, FF_W), lambda i, j: (0, 0)),
        ],
        out_specs=[
            pl.BlockSpec((tm, tn), lambda i, j: (i, j)),
            pl.BlockSpec((tm, FF_W), lambda i, j: (i, 0)),
        ],
        out_shape=[jax.ShapeDtypeStruct((t, PROJ_W), BF16), jax.ShapeDtypeStruct((t, FF_W), F32)],
        scratch_shapes=[pltpu.VMEM((tm, D_MODEL), BF16)],
        compiler_params=_params("parallel", "arbitrary"),
        name="in_proj",
    )(x2, g, b, w_main, w_ff)


def _retention_kernel(q_ref, k_ref, v_ref, g_ref, cos_ref, sin_ref, intra_ref, qd_ref, kd_ref,
                      o_ref, r_scr, *, chunk_decay):
    @pl.when(pl.program_id(1) == 0)
    def _():
        r_scr[...] = jnp.zeros_like(r_scr)

    for ci in range(q_ref.shape[0] // RET_CHUNK):
        rows = slice(ci * RET_CHUNK, (ci + 1) * RET_CHUNK)
        cos = cos_ref[rows, :]
        sin = sin_ref[rows, :]
        for h in range(RET_HEADS):
            q = q_ref[rows, h * RET_DK:(h + 1) * RET_DK].astype(F32)
            k = k_ref[rows, h * RET_DK:(h + 1) * RET_DK].astype(F32)
            v = v_ref[rows, h * RET_DV:(h + 1) * RET_DV]
            qr = q * cos + pltpu.roll(q, RET_DK // 2, 1) * sin
            kr = (k * cos + pltpu.roll(k, RET_DK // 2, 1) * sin) * (RET_DK ** -0.5)
            s = lax.dot_general(qr.astype(BF16), kr.astype(BF16), (((1,), (1,)), ((), ())),
                                preferred_element_type=F32) * intra_ref[h]
            inner = jnp.dot(s.astype(BF16), v, preferred_element_type=F32)
            r_old = r_scr[h]
            cross = jnp.dot((qr * qd_ref[h]).astype(BF16), r_old.astype(BF16), preferred_element_type=F32)
            kv = lax.dot_general((kr * kd_ref[h]).astype(BF16), v, (((0,), (0,)), ((), ())),
                                 preferred_element_type=F32)
            r_scr[h] = chunk_decay[h] * r_old + kv
            y = inner + cross
            mu = jnp.mean(y, axis=-1, keepdims=True)
            yc = y - mu
            var = jnp.mean(yc * yc, axis=-1, keepdims=True)
            yn = yc * lax.rsqrt(var + LN_EPS)
            gate = g_ref[rows, h * RET_DV:(h + 1) * RET_DV].astype(F32)
            o_ref[rows, h * RET_DV:(h + 1) * RET_DV] = (gate * jax.nn.sigmoid(gate) * yn).astype(BF16)


def _retention(proj, cos_t, sin_t, intra, qdec, kdec, chunk_decay, batch, seq, chunks_per_step=8):
    c = RET_CHUNK * chunks_per_step
    nc = seq // c
    row = lambda b, j: b * nc + j
    return pl.pallas_call(
        functools.partial(_retention_kernel, chunk_decay=chunk_decay),
        grid=(batch, nc),
        in_specs=[
            pl.BlockSpec((c, RET_QK_W), lambda b, j: (row(b, j), OFF_RQ // RET_QK_W)),
            pl.BlockSpec((c, RET_QK_W), lambda b, j: (row(b, j), OFF_RK // RET_QK_W)),
            pl.BlockSpec((c, RET_V_W), lambda b, j: (row(b, j), OFF_RV // RET_V_W)),
            pl.BlockSpec((c, RET_V_W), lambda b, j: (row(b, j), OFF_RG // RET_V_W)),
            pl.BlockSpec((c, RET_DK), lambda b, j: (j, 0)),
            pl.BlockSpec((c, RET_DK), lambda b, j: (j, 0)),
            pl.BlockSpec((RET_HEADS, RET_CHUNK, RET_CHUNK), lambda b, j: (0, 0, 0)),
            pl.BlockSpec((RET_HEADS, RET_CHUNK, RET_DK), lambda b, j: (0, 0, 0)),
            pl.BlockSpec((RET_HEADS, RET_CHUNK, RET_DK), lambda b, j: (0, 0, 0)),
        ],
        out_specs=pl.BlockSpec((c, RET_V_W), lambda b, j: (row(b, j), 0)),
        out_shape=jax.ShapeDtypeStruct((batch * seq, RET_V_W), BF16),
        scratch_shapes=[pltpu.VMEM((RET_HEADS, RET_DK, RET_DV), F32)],
        compiler_params=_params("parallel", "arbitrary"),
        name="retention",
    )(proj, proj, proj, proj, cos_t, sin_t, intra, qdec, kdec)


def _fcumsum_kernel(ff_ref, bf_ref, tri_ref, c_ref, ct_ref, *, blk):
    seq = ff_ref.shape[0]
    carry = jnp.zeros((1, FF_W), F32)
    for i in range(seq // blk):
        lf = jax.nn.log_sigmoid(ff_ref[i * blk:(i + 1) * blk, :] + bf_ref[...])
        cs = jnp.dot(tri_ref[...], lf, preferred_element_type=F32,
                     precision=lax.Precision.HIGHEST) + carry
        c_ref[i * blk:(i + 1) * blk, :] = cs
        carry = cs[blk - 1:blk, :]
        for p in range(FOX_HEADS // 2):
            ct_ref[0, p, i] = cs[:, p * 128:(p + 1) * 128].T[0:8, :]


def _fcumsum(ff, bf_pad, batch, seq, blk):
    tri = (lax.broadcasted_iota(jnp.int32, (blk, blk), 0)
           >= lax.broadcasted_iota(jnp.int32, (blk, blk), 1)).astype(F32)
    nb = seq // blk
    return pl.pallas_call(
        functools.partial(_fcumsum_kernel, blk=blk),
        grid=(batch,),
        in_specs=[
            pl.BlockSpec((seq, FF_W), lambda b: (b, 0)),
            pl.BlockSpec((1, FF_W), lambda b: (0, 0)),
            pl.BlockSpec((blk, blk), lambda b: (0, 0)),
        ],
        out_specs=[
            pl.BlockSpec((seq, FF_W), lambda b: (b, 0)),
            pl.BlockSpec((1, FOX_HEADS // 2, nb, 8, blk), lambda b: (b, 0, 0, 0, 0)),
        ],
        out_shape=[jax.ShapeDtypeStruct((batch * seq, FF_W), F32),
                   jax.ShapeDtypeStruct((batch, FOX_HEADS // 2, nb, 8, blk), F32)],
        compiler_params=_params("parallel"),
        name="fcumsum",
    )(ff, bf_pad, tri)


def _fox_kernel(q_ref, k_ref, v_ref, c_ref, ct_ref, o_ref, vt_scr, ckb_scr, m_scr, l_scr, acc_scr, *, tq):
    qi = pl.program_id(2)
    seq = k_ref.shape[0]
    nkv = seq // tq

    @pl.when(qi == 0)
    def _():
        for j in range(nkv):
            vt_scr[j] = v_ref[j * tq:(j + 1) * tq, :].astype(F32).T.astype(BF16)
        for r in range(2):
            ckb_scr[r] = jnp.broadcast_to(c_ref[:, r:r + 1], (seq, 128))

    lane = lax.broadcasted_iota(jnp.int32, (tq, 128), 1)
    key_i = lax.broadcasted_iota(jnp.int32, (tq, tq), 0)
    qry_i = lax.broadcasted_iota(jnp.int32, (tq, tq), 1)
    q = q_ref[...] * jnp.asarray(FOX_DH ** -0.5, BF16)
    qh = [jnp.where((lane >= r * FOX_DH) & (lane < (r + 1) * FOX_DH), q, jnp.zeros_like(q)) for r in range(2)]
    cq = [ct_ref[0, 0, qi, r:r + 1, :] for r in range(2)]
    m_scr[...] = jnp.full_like(m_scr, NEG_BIG)
    l_scr[...] = jnp.zeros_like(l_scr)
    acc_scr[...] = jnp.zeros_like(acc_scr)

    def block(j, diag):
        start = pl.multiple_of(j * tq, tq)
        kb = k_ref[pl.ds(start, tq), :]
        vtb = vt_scr[j]
        s = [lax.dot_general(kb, qh[r], (((1,), (1,)), ((), ())), preferred_element_type=F32)
             for r in range(2)]
        ck = [ckb_scr[r, pl.ds(start, tq), :] for r in range(2)]
        s = [s[r] + cq[r] - jnp.concatenate([ck[r]] * (tq // 128), axis=1) for r in range(2)]
        if diag:
            s = [jnp.where(key_i <= qry_i, s[r], -1e30) for r in range(2)]
        m_old = [m_scr[r] for r in range(2)]
        m_new = [jnp.maximum(m_old[r], jnp.max(s[r], axis=0, keepdims=True)) for r in range(2)]
        alpha = [jnp.exp(m_old[r] - m_new[r]) for r in range(2)]
        p = [jnp.exp(s[r] - m_new[r]) for r in range(2)]
        for r in range(2):
            l_scr[r] = alpha[r] * l_scr[r] + jnp.sum(p[r], axis=0, keepdims=True)
            pv = jnp.dot(vtb[r * FOX_DH:(r + 1) * FOX_DH, :], p[r].astype(BF16), preferred_element_type=F32)
            acc_scr[r] = alpha[r] * acc_scr[r] + pv
            m_scr[r] = m_new[r]

    def body(j, carry):
        block(j, False)
        return carry

    lax.fori_loop(0, qi, body, 0)
    block(qi, True)
    out_t = jnp.concatenate([acc_scr[r] / l_scr[r] for r in range(2)], axis=0)
    o_ref[...] = out_t.T.astype(BF16)


def _fox(proj, c, ct, batch, seq, tq):
    nq = seq // tq
    np_ = FOX_HEADS // 2
    return pl.pallas_call(
        functools.partial(_fox_kernel, tq=tq),
        grid=(batch, np_, nq),
        in_specs=[
            pl.BlockSpec((tq, 128), lambda b, p, i: (b * nq + i, OFF_FQ // 128 + p)),
            pl.BlockSpec((seq, 128), lambda b, p, i: (b, OFF_FK // 128 + p)),
            pl.BlockSpec((seq, 128), lambda b, p, i: (b, OFF_FV // 128 + p)),
            pl.BlockSpec((seq, 128), lambda b, p, i: (b, p)),
            pl.BlockSpec((1, 1, nq, 8, tq), lambda b, p, i: (b, p, 0, 0, 0)),
        ],
        out_specs=pl.BlockSpec((tq, 128), lambda b, p, i: (b * nq + i, p)),
        out_shape=jax.ShapeDtypeStruct((batch * seq, FOX_W), BF16),
        scratch_shapes=[
            pltpu.VMEM((nq, 128, tq), BF16),
            pltpu.VMEM((2, seq, 128), F32),
            pltpu.VMEM((2, 1, tq), F32),
            pltpu.VMEM((2, 1, tq), F32),
            pltpu.VMEM((2, FOX_DH, tq), F32),
        ],
        compiler_params=_params("parallel", "parallel", "arbitrary"),
        name="fox",
    )(proj, proj, proj, c, ct)


def _mix_kernel(x_ref, yr_ref, yf_ref, gr_ref, gf_ref, p_ref, ge_ref, be_ref, bg_ref, wro_ref, wfo_ref,
                wout_ref, g1_ref, b1_ref, wpg_ref, bpg_ref, wple_ref, base_ref, ht_ref, *, alpha):
    h0 = _layer_norm(x_ref[...], ge_ref[...], be_ref[...])
    y_ret = jnp.dot(yr_ref[...], wro_ref[...], preferred_element_type=F32)
    y_fox = jnp.dot(yf_ref[...], wfo_ref[...], preferred_element_type=F32)
    merged = (jax.nn.sigmoid(gr_ref[...].astype(F32) + bg_ref[0:1, :]) * y_ret
              + jax.nn.sigmoid(gf_ref[...].astype(F32) + bg_ref[1:2, :]) * y_fox)
    out = jnp.dot(merged.astype(BF16), wout_ref[...], preferred_element_type=F32)
    h1 = _layer_norm(alpha * h0 + out, g1_ref[...], b1_ref[...])
    h1b = h1.astype(BF16)
    gate = jax.nn.sigmoid(jnp.dot(h1b, wpg_ref[...], preferred_element_type=F32) + bpg_ref[...])
    ple = gate * jnp.dot(p_ref[...].astype(BF16), wple_ref[...], preferred_element_type=F32)
    base_ref[...] = alpha * h1 + ple
    ht_ref[0] = h1.T.astype(BF16)


def _mix(x2, yr, yf, proj, p2, ge, be, bg, wro, wfo, wout, g1, b1, wpg, bpg, wple, alpha, tm=512):
    t = x2.shape[0]
    full = lambda shape: pl.BlockSpec(shape, lambda i: (0,) * len(shape))
    return pl.pallas_call(
        functools.partial(_mix_kernel, alpha=alpha),
        grid=(t // tm,),
        in_specs=[
            pl.BlockSpec((tm, D_MODEL), lambda i: (i, 0)),
            pl.BlockSpec((tm, RET_V_W), lambda i: (i, 0)),
            pl.BlockSpec((tm, FOX_W), lambda i: (i, 0)),
            pl.BlockSpec((tm, D_MODEL), lambda i: (i, OFF_GR // D_MODEL)),
            pl.BlockSpec((tm, D_MODEL), lambda i: (i, OFF_GF // D_MODEL)),
            pl.BlockSpec((tm, PLE_DIM), lambda i: (i, 0)),
            full((1, D_MODEL)), full((1, D_MODEL)), full((2, D_MODEL)),
            full((RET_V_W, D_MODEL)), full((FOX_W, D_MODEL)), full((D_MODEL, D_MODEL)),
            full((1, D_MODEL)), full((1, D_MODEL)),
            full((D_MODEL, D_MODEL)), full((1, D_MODEL)), full((PLE_DIM, D_MODEL)),
        ],
        out_specs=[
            pl.BlockSpec((tm, D_MODEL), lambda i: (i, 0)),
            pl.BlockSpec((1, D_MODEL, tm), lambda i: (i, 0, 0)),
        ],
        out_shape=[jax.ShapeDtypeStruct((t, D_MODEL), F32),
                   jax.ShapeDtypeStruct((t // tm, D_MODEL, tm), BF16)],
        compiler_params=_params("parallel"),
        name="mix",
    )(x2, yr, yf, proj, proj, p2, ge, be, bg, wro, wfo, wout, g1, b1, wpg, bpg, wple)


def _top16(sc):
    nk = sc.shape[0]
    iota = lax.broadcasted_iota(jnp.int32, sc.shape, 0).astype(F32)
    rank = jnp.full(sc.shape, float(PEER_TOPK), F32)
    vals = []
    for a in range(PEER_TOPK):
        m = jnp.max(sc, axis=0, keepdims=True)
        first = jnp.min(jnp.where(sc == m, iota, float(nk)), axis=0, keepdims=True)
        hit = iota == first
        sc = jnp.where(hit, NEG_BIG, sc)
        rank = jnp.where(hit, float(a), rank)
        vals.append(m)
    return vals, rank


def _mark(a):
    return MARK_BASE * (1.0 + (a + 1) / 64.0)


def _top16_no_ties(chains):
    vals = [[] for _ in chains]
    for a in range(PEER_TOPK):
        ms = [jnp.max(sc, axis=0, keepdims=True) for sc in chains]
        chains = [jnp.where(sc == m, _mark(a), sc) for sc, m in zip(chains, ms)]
        for v, m in zip(vals, ms):
            v.append(m)
    return vals, chains


def _route_kernel(ht_ref, wq_ref, keys_ref, ct_ref, w0_ref, r1_ref, w1_ref,
                  q_scr, sc_scr, rank_scr, val_scr, cnt_scr, z_scr, *, lanes):
    tb = ht_ref.shape[2]
    groups = [slice(g * lanes, (g + 1) * lanes) for g in range(tb // lanes)]
    q_scr[...] = jnp.dot(wq_ref[...], ht_ref[0], preferred_element_type=F32).astype(BF16)
    val_scr[...] = jnp.zeros_like(val_scr)

    for hc in range(2 * PEER_HEADS):
        sc_scr[hc] = jnp.dot(keys_ref[hc], q_scr[hc * PEER_HALF:(hc + 1) * PEER_HALF, :],
                             preferred_element_type=F32)

    def store_vals(c, h, sl, vals):
        own_row = lax.broadcasted_iota(jnp.int32, (PEER_HEADS, lanes), 0) == h
        for a in range(PEER_TOPK):
            val_scr[c, a, :, sl] = jnp.where(own_row, vals[a], val_scr[c, a, :, sl])

    def select_cells(exact):
        flat = [float(a * PEER_TOPK + b) for a, b in CAND_CELLS]
        excess = jnp.zeros((PEER_HEADS, lanes), F32)
        for sl in groups:
            v0 = [val_scr[0, a, :, sl] for a in range(PEER_TOPK)]
            v1 = [val_scr[1, b, :, sl] for b in range(PEER_TOPK)]
            cand = [v0[a] + v1[b] for a, b in CAND_CELLS]
            if exact:
                sel = [jnp.zeros_like(v0[0]) for _ in CAND_CELLS]
                for _ in range(PEER_TOPK):
                    m = functools.reduce(jnp.maximum, cand)
                    first = functools.reduce(
                        jnp.minimum, [jnp.where(cv == m, fi, 1e9) for cv, fi in zip(cand, flat)])
                    hits = [first == fi for fi in flat]
                    cand = [jnp.where(hit, NEG_BIG, cv) for hit, cv in zip(hits, cand)]
                    sel = [jnp.where(hit, 1.0, sv) for hit, sv in zip(hits, sel)]
            else:
                thr = functools.reduce(jnp.maximum, cand)
                for _ in range(PEER_TOPK - 1):
                    thr = functools.reduce(jnp.maximum, [jnp.where(cv < thr, cv, NEG_BIG) for cv in cand])
                sel = [jnp.where(cv >= thr, 1.0, 0.0) for cv in cand]
            e0 = [jnp.exp(v0[a] - v0[0]) for a in range(PEER_TOPK)]
            e1 = [jnp.exp(v1[b] - v1[0]) for b in range(PEER_TOPK)]
            z = jnp.zeros_like(v0[0])
            cnt = [jnp.zeros_like(v0[0]) for _ in range(PEER_TOPK)]
            for t, (a, b) in enumerate(CAND_CELLS):
                z = z + sel[t] * (e0[a] * e1[b])
                cnt[a] = cnt[a] + sel[t]
            z_scr[:, sl] = 1.0 / z
            for a in range(PEER_TOPK):
                cnt_scr[a, :, sl] = cnt[a]
            excess = jnp.maximum(excess, jnp.abs(functools.reduce(jnp.add, sel) - float(PEER_TOPK)))
        return excess

    def fast_head(h, bad):
        for sl in groups:
            vals, marked = _top16_no_ties([sc_scr[2 * h, :, sl], sc_scr[2 * h + 1, :, sl]])
            for c in range(2):
                taken = marked[c] <= MARK_BASE
                unit = marked[c] * -(2.0 ** -100) * (2.0 ** -27)
                rank_scr[2 * h + c, :, sl] = jnp.where(taken, (unit - 1.0) * 64.0 - 1.0, float(PEER_TOPK))
                n_taken = jnp.sum(jnp.where(taken, 1.0, 0.0), axis=0, keepdims=True)
                bad = jnp.maximum(bad, jnp.abs(n_taken - float(PEER_TOPK)))
                store_vals(c, h, sl, vals[c])
        return bad

    bad = jnp.max(lax.fori_loop(0, PEER_HEADS, fast_head, jnp.zeros((1, lanes), F32)))
    bad = jnp.maximum(bad, jnp.max(select_cells(False)))

    @pl.when(bad > 0.0)
    def _():
        def exact_vec(hc, carry):
            for sl in groups:
                vals, rank = _top16(sc_scr[hc, :, sl])
                rank_scr[hc, :, sl] = rank
                store_vals(hc % 2, hc // 2, sl, vals)
            return carry

        lax.fori_loop(0, 2 * PEER_HEADS, exact_vec, 0)
        select_cells(True)

    for h in range(PEER_HEADS):
        s0 = sc_scr[2 * h]
        s1 = sc_scr[2 * h + 1]
        rank0 = rank_scr[2 * h]
        ct = jnp.zeros_like(s0)
        for a in range(PEER_TOPK):
            ct = jnp.where(rank0 == float(a), cnt_scr[a, h:h + 1, :], ct)
        ct_ref[0, h] = ct
        w0_ref[0, h] = jnp.exp(s0 - val_scr[0, 0, h:h + 1, :]) * z_scr[h:h + 1, :]
        w1_ref[0, h] = jnp.exp(s1 - val_scr[1, 0, h:h + 1, :]).astype(BF16)
        r1_ref[0, h] = rank_scr[2 * h + 1].astype(BF16)


def _route(ht, wq_t, keys, tb=256, lanes=128):
    tm = ht.shape[2]
    t = ht.shape[0] * tm
    per = tm // tb
    nq = 2 * PEER_HEADS * PEER_HALF
    routed = lambda dt: jax.ShapeDtypeStruct((t // tb, PEER_HEADS, PEER_NKEYS, tb), dt)
    rspec = pl.BlockSpec((1, PEER_HEADS, PEER_NKEYS, tb), lambda n: (n, 0, 0, 0))
    return pl.pallas_call(
        functools.partial(_route_kernel, lanes=lanes),
        grid=(t // tb,),
        in_specs=[
            pl.BlockSpec((1, D_MODEL, tb), lambda n: (n // per, 0, n % per)),
            pl.BlockSpec((nq, D_MODEL), lambda n: (0, 0)),
            pl.BlockSpec((2 * PEER_HEADS, PEER_NKEYS, PEER_HALF), lambda n: (0, 0, 0)),
        ],
        out_specs=[rspec, rspec, rspec, rspec],
        out_shape=[routed(F32), routed(F32), routed(BF16), routed(BF16)],
        scratch_shapes=[
            pltpu.VMEM((nq, tb), BF16),
            pltpu.VMEM((2 * PEER_HEADS, PEER_NKEYS, tb), F32),
            pltpu.VMEM((2 * PEER_HEADS, PEER_NKEYS, tb), F32),
            pltpu.VMEM((2, PEER_TOPK, PEER_HEADS, tb), F32),
            pltpu.VMEM((PEER_TOPK, PEER_HEADS, tb), F32),
            pltpu.VMEM((PEER_HEADS, tb), F32),
        ],
        compiler_params=_params("parallel"),
        name="route",
    )(ht, wq_t, keys)


def _peer_kernel(ht_ref, u_ref, vt_ref, ct_ref, w0_ref, r1_ref, w1_ref, base_ref, g2_ref, b2_ref,
                 o_ref, hid_scr, a_scr, acc_scr, r1_scr, w1_scr, *, eb, lanes):
    e = pl.program_id(1)
    tb = ht_ref.shape[2]
    tr = r1_ref.shape[3]
    sub = 16

    @pl.when(e == 0)
    def _():
        acc_scr[...] = jnp.zeros_like(acc_scr)
        for rb in range(tb // tr):
            r1_scr[:, :, rb * tr:(rb + 1) * tr] = r1_ref[rb]
            w1_scr[:, :, lanes + rb * tr:lanes + (rb + 1) * tr] = w1_ref[rb]

    hid_scr[...] = jnp.dot(u_ref[...], ht_ref[0], preferred_element_type=F32)
    for il in range(eb // PEER_NKEYS):
        for lg in range(tb // lanes):
            sl = slice(lg * lanes, (lg + 1) * lanes)
            sw = slice((lg + 1) * lanes, (lg + 2) * lanes)
            rb, rl = divmod(lg * lanes, tr)
            st = slice(rl, rl + lanes)
            cts = [jnp.broadcast_to(ct_ref[rb, h, il:il + 1, st], (sub, lanes)).astype(BF16)
                   for h in range(PEER_HEADS)]
            w0s = [jnp.broadcast_to(0.5 * w0_ref[rb, h, il:il + 1, st], (sub, lanes)).astype(BF16)
                   for h in range(PEER_HEADS)]
            for jb in range(PEER_NKEYS // sub):
                js = slice(jb * sub, (jb + 1) * sub)
                gsum = None
                for h in range(PEER_HEADS):
                    gate0 = jnp.minimum(jnp.maximum(cts[h] - r1_scr[h, js, sl], 0.0), w0s[h])
                    term = gate0 * w1_scr[h, js, sw]
                    gsum = term if gsum is None else gsum + term
                rows = slice(il * PEER_NKEYS + jb * sub, il * PEER_NKEYS + (jb + 1) * sub)
                hv = hid_scr[rows, sl].astype(BF16)
                act = hv * (1.0 + lax.erf(hv * jnp.asarray(1.0 / math.sqrt(2.0), BF16)))
                a_scr[rows, sl] = act * gsum
    acc_scr[...] += jnp.dot(vt_ref[...], a_scr[...], preferred_element_type=F32)

    @pl.when(e == pl.num_programs(1) - 1)
    def _():
        o_ref[...] = _layer_norm(base_ref[...] + acc_scr[...].T, g2_ref[...], b2_ref[...])


def _peer(ht, u_bf, vt_bf, ct, w0, r1, w1, base, g2, b2, eb=2048, lanes=128):
    tb = ht.shape[2]
    t = ht.shape[0] * tb
    kr = eb // PEER_NKEYS
    tr = r1.shape[3]
    rspec = pl.BlockSpec((tb // tr, PEER_HEADS, PEER_NKEYS, tr), lambda n, e: (n, 0, 0, 0))
    kspec = pl.BlockSpec((tb // tr, PEER_HEADS, kr, tr), lambda n, e: (n, 0, e, 0))
    return pl.pallas_call(
        functools.partial(_peer_kernel, eb=eb, lanes=lanes),
        grid=(t // tb, PEER_NEXP // eb),
        in_specs=[
            pl.BlockSpec((1, D_MODEL, tb), lambda n, e: (n, 0, 0)),
            pl.BlockSpec((eb, D_MODEL), lambda n, e: (e, 0)),
            pl.BlockSpec((D_MODEL, eb), lambda n, e: (0, e)),
            kspec, kspec, rspec, rspec,
            pl.BlockSpec((tb, D_MODEL), lambda n, e: (n, 0)),
            pl.BlockSpec((1, D_MODEL), lambda n, e: (0, 0)),
            pl.BlockSpec((1, D_MODEL), lambda n, e: (0, 0)),
        ],
        out_specs=pl.BlockSpec((tb, D_MODEL), lambda n, e: (n, 0)),
        out_shape=jax.ShapeDtypeStruct((t, D_MODEL), F32),
        scratch_shapes=[
            pltpu.VMEM((eb, tb), F32),
            pltpu.VMEM((eb, tb), BF16),
            pltpu.VMEM((D_MODEL, tb), F32),
            pltpu.VMEM((PEER_HEADS, PEER_NKEYS, tb + lanes), BF16),
            pltpu.VMEM((PEER_HEADS, PEER_NKEYS, tb + lanes), BF16),
        ],
        compiler_params=_params("parallel", "arbitrary"),
        name="peer",
    )(ht, u_bf, vt_bf, ct, w0, r1, w1, base, g2, b2)


def _retention_tables(seq):
    half = RET_DK // 2
    inv = ROPE_BASE ** (-jnp.arange(half, dtype=F32) / half)
    ang = jnp.arange(seq, dtype=jnp.int32).astype(F32)[:, None] * inv[None, :]
    cos, sin = jnp.cos(ang), jnp.sin(ang)
    cos_t = jnp.concatenate([cos, cos], axis=-1)
    sin_t = jnp.concatenate([-sin, sin], axis=-1)
    c = RET_CHUNK
    log_g = jnp.log(1.0 - 2.0 ** (-5.0 - jnp.arange(RET_HEADS, dtype=F32)))
    idx = jnp.arange(c, dtype=F32)
    diff = idx[:, None] - idx[None, :]
    intra = jnp.where(diff >= 0, jnp.exp(log_g[:, None, None] * jnp.maximum(diff, 0.0)), 0.0)
    qdec = jnp.broadcast_to(jnp.exp(log_g[:, None] * (idx + 1.0))[..., None], (RET_HEADS, c, RET_DK))
    kdec = jnp.broadcast_to(jnp.exp(log_g[:, None] * (c - 1.0 - idx))[..., None], (RET_HEADS, c, RET_DK))
    chunk_decay = tuple(math.exp(math.log(1.0 - 2.0 ** (-5.0 - h)) * c) for h in range(RET_HEADS))
    return cos_t, sin_t, intra, qdec, kdec, chunk_decay


def _pack_w_in(w):
    rq, rk, rv, rg, fq, fk, fv, ff, gr, gf = jnp.split(
        w, [512, 1024, 2048, 3072, 3584, 4096, 4608, 4616, 5640], axis=-1)
    main = jnp.concatenate([rq, rk, rv, rg, gr, gf, fq, fk, fv], axis=-1).astype(BF16)
    ff_pad = jnp.zeros((w.shape[0], FOX_HEADS // 2, 128), w.dtype).at[:, :, 0:2].set(
        ff.reshape(w.shape[0], FOX_HEADS // 2, 2)).reshape(w.shape[0], FF_W).astype(BF16)
    return main, ff_pad


def kernel(x, p, ln_emb_g, ln_emb_b, w_in, b_forget, b_branch_gate, w_ret_o, w_fox_o, w_out, ln1_g, ln1_b,
           w_peer_q, peer_sub_keys, peer_u, peer_v, w_ple_gate, b_ple_gate, w_ple, ln2_g, ln2_b):
    batch, seq, d = x.shape
    depth = w_in.shape[0]
    assert depth == 1 and d == D_MODEL, "the trunk-entry norm is fused into the single layer"
    t = batch * seq
    alpha = (2.0 * depth) ** 0.25
    fox_tq = min(512, seq)
    row = lambda v: v.reshape(1, -1).astype(F32)
    cos_t, sin_t, intra, qdec, kdec, chunk_decay = _retention_tables(seq)

    x2 = x.reshape(t, d)
    ge, be = row(ln_emb_g), row(ln_emb_b)
    w_main, w_ff = _pack_w_in(w_in[0])
    bf_pad = jnp.zeros((FOX_HEADS // 2, 128), F32).at[:, 0:2].set(
        b_forget[0].astype(F32).reshape(FOX_HEADS // 2, 2)).reshape(1, FF_W)
    proj, ff = _in_proj(x2, ge, be, w_main, w_ff)
    yr = _retention(proj, cos_t, sin_t, intra, qdec, kdec, chunk_decay, batch, seq)
    c, ct = _fcumsum(ff, bf_pad, batch, seq, fox_tq)
    yf = _fox(proj, c, ct, batch, seq, fox_tq)
    base, ht = _mix(x2, yr, yf, proj, p[0].reshape(t, PLE_DIM), ge, be, b_branch_gate[0].astype(F32),
                    w_ret_o[0].astype(BF16), w_fox_o[0].astype(BF16), w_out[0].astype(BF16),
                    row(ln1_g[0]), row(ln1_b[0]), w_ple_gate[0].astype(BF16), row(b_ple_gate[0]),
                    w_ple[0].astype(BF16), alpha)
    keys = peer_sub_keys[0].reshape(2 * PEER_HEADS, PEER_NKEYS, PEER_HALF).astype(BF16)
    ct_r, w0, r1, w1 = _route(ht, w_peer_q[0].T.astype(BF16), keys)
    out = _peer(ht, peer_u[0].astype(BF16), peer_v[0].T.astype(BF16), ct_r, w0, r1, w1, base,
                row(ln2_g[0]), row(ln2_b[0]))
    return out.reshape(batch, seq, d)
```

```python
import functools
import math

import jax
import jax.numpy as jnp
from jax import lax
from jax.experimental import pallas as pl
from jax.experimental.pallas import tpu as pltpu

F32 = jnp.float32
BF16 = jnp.bfloat16

D_MODEL = 1024
RET_HEADS = 4
RET_DK = 128
RET_DV = 256
RET_CHUNK = 128
FOX_HEADS = 8
FOX_DH = 64
PEER_HEADS = 8
PEER_NKEYS = 128
PEER_NEXP = PEER_NKEYS * PEER_NKEYS
PEER_HALF = 128
PEER_TOPK = 16
PLE_DIM = 256
LN_EPS = 1e-5
ROPE_BASE = 10000.0

RET_QK_W = RET_HEADS * RET_DK
RET_V_W = RET_HEADS * RET_DV
FOX_W = FOX_HEADS * FOX_DH

OFF_RQ, OFF_RK, OFF_RV, OFF_RG = 0, 512, 1024, 2048
OFF_GR, OFF_GF = 3072, 4096
OFF_FQ, OFF_FK, OFF_FV = 5120, 5632, 6144
PROJ_W = 6656
FF_IN = 128
FF_W = 512

V7X_VMEM_LIMIT = 56 * 1024 * 1024
NEG_BIG = -3.0e38
MARK_BASE = -(2.0 ** 127)
CAND_CELLS = tuple((a, b) for a in range(PEER_TOPK) for b in range(PEER_TOPK)
                   if (a + 1) * (b + 1) <= PEER_TOPK)


def _layer_norm(x, g, b):
    mu = jnp.mean(x, axis=-1, keepdims=True)
    xc = x - mu
    var = jnp.mean(xc * xc, axis=-1, keepdims=True)
    return xc * lax.rsqrt(var + LN_EPS) * g + b


def _params(*sem):
    return pltpu.CompilerParams(dimension_semantics=sem, vmem_limit_bytes=V7X_VMEM_LIMIT)


def _inproj_kernel(x_ref, g_ref, b_ref, w_ref, wff_ref, o_ref, ff_ref, h_scr):
    @pl.when(pl.program_id(1) == 0)
    def _():
        hb = _layer_norm(x_ref[...], g_ref[...], b_ref[...]).astype(BF16)
        h_scr[...] = hb
        ff_ref[...] = jnp.dot(hb, wff_ref[...], preferred_element_type=F32)

    o_ref[...] = jnp.dot(h_scr[...], w_ref[...], preferred_element_type=F32).astype(BF16)


def _in_proj(x2, g, b, w_main, w_ff, tm=1024, tn=3328):
    t = x2.shape[0]
    return pl.pallas_call(
        _inproj_kernel,
        grid=(t // tm, PROJ_W // tn),
        in_specs=[
            pl.BlockSpec((tm, D_MODEL), lambda i, j: (i, 0)),
            pl.BlockSpec((1, D_MODEL), lambda i, j: (0, 0)),
            pl.BlockSpec((1, D_MODEL), lambda i, j: (0, 0)),
            pl.BlockSpec((D_MODEL, tn), lambda i, j: (0, j)),
            pl.BlockSpec((D_MODEL, FF_IN), lambda i, j: (0, 0)),
        ],
        out_specs=[
            pl.BlockSpec((tm, tn), lambda i, j: (i, j)),
            pl.BlockSpec((tm, FF_IN), lambda i, j: (i, 0)),
        ],
        out_shape=[jax.ShapeDtypeStruct((t, PROJ_W), BF16), jax.ShapeDtypeStruct((t, FF_IN), F32)],
        scratch_shapes=[pltpu.VMEM((tm, D_MODEL), BF16)],
        compiler_params=_params("parallel", "arbitrary"),
        name="in_proj",
    )(x2, g, b, w_main, w_ff)


def _retention_kernel(q_ref, k_ref, v_ref, g_ref, cos_ref, sin_ref, intra_ref, qd_ref, kd_ref,
                      o_ref, r_scr, *, chunk_decay):
    @pl.when(pl.program_id(1) == 0)
    def _():
        r_scr[...] = jnp.zeros_like(r_scr)

    for ci in range(q_ref.shape[0] // RET_CHUNK):
        rows = slice(ci * RET_CHUNK, (ci + 1) * RET_CHUNK)
        cos = cos_ref[rows, :]
        sin = sin_ref[rows, :]
        for h in range(RET_HEADS):
            q = q_ref[rows, h * RET_DK:(h + 1) * RET_DK].astype(F32)
            k = k_ref[rows, h * RET_DK:(h + 1) * RET_DK].astype(F32)
            v = v_ref[rows, h * RET_DV:(h + 1) * RET_DV]
            qr = q * cos + pltpu.roll(q, RET_DK // 2, 1) * sin
            kr = (k * cos + pltpu.roll(k, RET_DK // 2, 1) * sin) * (RET_DK ** -0.5)
            s = lax.dot_general(qr.astype(BF16), kr.astype(BF16), (((1,), (1,)), ((), ())),
                                preferred_element_type=F32) * intra_ref[h]
            inner = jnp.dot(s.astype(BF16), v, preferred_element_type=F32)
            r_old = r_scr[h]
            cross = jnp.dot((qr * qd_ref[h]).astype(BF16), r_old.astype(BF16), preferred_element_type=F32)
            kv = lax.dot_general((kr * kd_ref[h]).astype(BF16), v, (((0,), (0,)), ((), ())),
                                 preferred_element_type=F32)
            r_scr[h] = chunk_decay[h] * r_old + kv
            y = inner + cross
            mu = jnp.mean(y, axis=-1, keepdims=True)
            yc = y - mu
            var = jnp.mean(yc * yc, axis=-1, keepdims=True)
            yn = yc * lax.rsqrt(var + LN_EPS)
            gate = g_ref[rows, h * RET_DV:(h + 1) * RET_DV].astype(F32)
            o_ref[rows, h * RET_DV:(h + 1) * RET_DV] = (gate * jax.nn.sigmoid(gate) * yn).astype(BF16)


def _retention(proj, cos_t, sin_t, intra, qdec, kdec, chunk_decay, batch, seq, chunks_per_step=8):
    c = RET_CHUNK * chunks_per_step
    nc = seq // c
    row = lambda b, j: b * nc + j
    return pl.pallas_call(
        functools.partial(_retention_kernel, chunk_decay=chunk_decay),
        grid=(batch, nc),
        in_specs=[
            pl.BlockSpec((c, RET_QK_W), lambda b, j: (row(b, j), OFF_RQ // RET_QK_W)),
            pl.BlockSpec((c, RET_QK_W), lambda b, j: (row(b, j), OFF_RK // RET_QK_W)),
            pl.BlockSpec((c, RET_V_W), lambda b, j: (row(b, j), OFF_RV // RET_V_W)),
            pl.BlockSpec((c, RET_V_W), lambda b, j: (row(b, j), OFF_RG // RET_V_W)),
            pl.BlockSpec((c, RET_DK), lambda b, j: (j, 0)),
            pl.BlockSpec((c, RET_DK), lambda b, j: (j, 0)),
            pl.BlockSpec((RET_HEADS, RET_CHUNK, RET_CHUNK), lambda b, j: (0, 0, 0)),
            pl.BlockSpec((RET_HEADS, RET_CHUNK, RET_DK), lambda b, j: (0, 0, 0)),
            pl.BlockSpec((RET_HEADS, RET_CHUNK, RET_DK), lambda b, j: (0, 0, 0)),
        ],
        out_specs=pl.BlockSpec((c, RET_V_W), lambda b, j: (row(b, j), 0)),
        out_shape=jax.ShapeDtypeStruct((batch * seq, RET_V_W), BF16),
        scratch_shapes=[pltpu.VMEM((RET_HEADS, RET_DK, RET_DV), F32)],
        compiler_params=_params("parallel", "arbitrary"),
        name="retention",
    )(proj, proj, proj, proj, cos_t, sin_t, intra, qdec, kdec)


def _fcumsum_kernel(ff_ref, bf_ref, tri_ref, c_ref, ct_ref, *, blk):
    seq = ff_ref.shape[0]
    carry = jnp.zeros((1, FF_IN), F32)
    for i in range(seq // blk):
        lf = jax.nn.log_sigmoid(ff_ref[i * blk:(i + 1) * blk, :] + bf_ref[...])
        cs = jnp.dot(tri_ref[...], lf, preferred_element_type=F32,
                     precision=lax.Precision.HIGHEST) + carry
        carry = cs[blk - 1:blk, :]
        for p in range(FOX_HEADS // 2):
            pair = cs if p == 0 else pltpu.roll(cs, FF_IN - 2 * p, 1)
            c_ref[i * blk:(i + 1) * blk, p * 128:(p + 1) * 128] = pair
            ct_ref[0, p, i] = pair.T[0:8, :]


def _fcumsum(ff, bf_pad, batch, seq, blk):
    tri = (lax.broadcasted_iota(jnp.int32, (blk, blk), 0)
           >= lax.broadcasted_iota(jnp.int32, (blk, blk), 1)).astype(F32)
    nb = seq // blk
    return pl.pallas_call(
        functools.partial(_fcumsum_kernel, blk=blk),
        grid=(batch,),
        in_specs=[
            pl.BlockSpec((seq, FF_IN), lambda b: (b, 0)),
            pl.BlockSpec((1, FF_IN), lambda b: (0, 0)),
            pl.BlockSpec((blk, blk), lambda b: (0, 0)),
        ],
        out_specs=[
            pl.BlockSpec((seq, FF_W), lambda b: (b, 0)),
            pl.BlockSpec((1, FOX_HEADS // 2, nb, 8, blk), lambda b: (b, 0, 0, 0, 0)),
        ],
        out_shape=[jax.ShapeDtypeStruct((batch * seq, FF_W), F32),
                   jax.ShapeDtypeStruct((batch, FOX_HEADS // 2, nb, 8, blk), F32)],
        compiler_params=_params("parallel"),
        name="fcumsum",
    )(ff, bf_pad, tri)


def _fox_kernel(q_ref, k_ref, v_ref, c_ref, ct_ref, o_ref, vt_scr, ckb_scr, m_scr, l_scr, acc_scr, *, tq):
    qi = pl.program_id(2)
    seq = k_ref.shape[0]
    nkv = seq // tq

    @pl.when(qi == 0)
    def _():
        for j in range(nkv):
            vt_scr[j] = v_ref[j * tq:(j + 1) * tq, :].astype(F32).T.astype(BF16)
        for r in range(2):
            ckb_scr[r] = jnp.broadcast_to(c_ref[:, r:r + 1], (seq, 128))

    lane = lax.broadcasted_iota(jnp.int32, (tq, 128), 1)
    key_i = lax.broadcasted_iota(jnp.int32, (tq, tq), 0)
    qry_i = lax.broadcasted_iota(jnp.int32, (tq, tq), 1)
    q = q_ref[...] * jnp.asarray(FOX_DH ** -0.5, BF16)
    qh = [jnp.where((lane >= r * FOX_DH) & (lane < (r + 1) * FOX_DH), q, jnp.zeros_like(q)) for r in range(2)]
    cq = [ct_ref[0, 0, qi, r:r + 1, :] for r in range(2)]
    m_scr[...] = jnp.full_like(m_scr, NEG_BIG)
    l_scr[...] = jnp.zeros_like(l_scr)
    acc_scr[...] = jnp.zeros_like(acc_scr)

    def block(j, diag):
        start = pl.multiple_of(j * tq, tq)
        kb = k_ref[pl.ds(start, tq), :]
        vtb = vt_scr[j]
        s = [lax.dot_general(kb, qh[r], (((1,), (1,)), ((), ())), preferred_element_type=F32)
             for r in range(2)]
        ck = [ckb_scr[r, pl.ds(start, tq), :] for r in range(2)]
        s = [s[r] - jnp.concatenate([ck[r]] * (tq // 128), axis=1) for r in range(2)]
        if diag:
            s = [jnp.where(key_i <= qry_i, s[r], -1e30) for r in range(2)]
        m_old = [m_scr[r] for r in range(2)]
        m_new = [jnp.maximum(m_old[r], jnp.max(s[r], axis=0, keepdims=True) + cq[r]) for r in range(2)]
        alpha = [jnp.exp(m_old[r] - m_new[r]) for r in range(2)]
        p = [jnp.exp(s[r] + (cq[r] - m_new[r])) for r in range(2)]
        for r in range(2):
            l_scr[r] = alpha[r] * l_scr[r] + jnp.sum(p[r], axis=0, keepdims=True)
            pv = jnp.dot(vtb[r * FOX_DH:(r + 1) * FOX_DH, :], p[r].astype(BF16), preferred_element_type=F32)
            acc_scr[r] = alpha[r] * acc_scr[r] + pv
            m_scr[r] = m_new[r]

    def body(j, carry):
        block(j, False)
        return carry

    lax.fori_loop(0, qi, body, 0)
    block(qi, True)
    out_t = jnp.concatenate([acc_scr[r] / l_scr[r] for r in range(2)], axis=0)
    o_ref[...] = out_t.T.astype(BF16)


def _fox(proj, c, ct, batch, seq, tq):
    nq = seq // tq
    np_ = FOX_HEADS // 2
    return pl.pallas_call(
        functools.partial(_fox_kernel, tq=tq),
        grid=(batch, np_, nq),
        in_specs=[
            pl.BlockSpec((tq, 128), lambda b, p, i: (b * nq + i, OFF_FQ // 128 + p)),
            pl.BlockSpec((seq, 128), lambda b, p, i: (b, OFF_FK // 128 + p)),
            pl.BlockSpec((seq, 128), lambda b, p, i: (b, OFF_FV // 128 + p)),
            pl.BlockSpec((seq, 128), lambda b, p, i: (b, p)),
            pl.BlockSpec((1, 1, nq, 8, tq), lambda b, p, i: (b, p, 0, 0, 0)),
        ],
        out_specs=pl.BlockSpec((tq, 128), lambda b, p, i: (b * nq + i, p)),
        out_shape=jax.ShapeDtypeStruct((batch * seq, FOX_W), BF16),
        scratch_shapes=[
            pltpu.VMEM((nq, 128, tq), BF16),
            pltpu.VMEM((2, seq, 128), F32),
            pltpu.VMEM((2, 1, tq), F32),
            pltpu.VMEM((2, 1, tq), F32),
            pltpu.VMEM((2, FOX_DH, tq), F32),
        ],
        compiler_params=_params("parallel", "parallel", "arbitrary"),
        name="fox",
    )(proj, proj, proj, c, ct)


def _mix_kernel(x_ref, yr_ref, yf_ref, gr_ref, gf_ref, p_ref, ge_ref, be_ref, bg_ref, wro_ref, wfo_ref,
                wout_ref, g1_ref, b1_ref, wpg_ref, bpg_ref, wple_ref, base_ref, ht_ref, *, alpha):
    h0 = _layer_norm(x_ref[...], ge_ref[...], be_ref[...])
    y_ret = jnp.dot(yr_ref[...], wro_ref[...], preferred_element_type=F32)
    y_fox = jnp.dot(yf_ref[...], wfo_ref[...], preferred_element_type=F32)
    merged = (jax.nn.sigmoid(gr_ref[...].astype(F32) + bg_ref[0:1, :]) * y_ret
              + jax.nn.sigmoid(gf_ref[...].astype(F32) + bg_ref[1:2, :]) * y_fox)
    out = jnp.dot(merged.astype(BF16), wout_ref[...], preferred_element_type=F32)
    h1 = _layer_norm(alpha * h0 + out, g1_ref[...], b1_ref[...])
    h1b = h1.astype(BF16)
    gate = jax.nn.sigmoid(jnp.dot(h1b, wpg_ref[...], preferred_element_type=F32) + bpg_ref[...])
    ple = gate * jnp.dot(p_ref[...].astype(BF16), wple_ref[...], preferred_element_type=F32)
    base_ref[...] = alpha * h1 + ple
    ht_ref[0] = h1.T.astype(BF16)


def _mix(x2, yr, yf, proj, p2, ge, be, bg, wro, wfo, wout, g1, b1, wpg, bpg, wple, alpha, tm=512):
    t = x2.shape[0]
    full = lambda shape: pl.BlockSpec(shape, lambda i: (0,) * len(shape))
    return pl.pallas_call(
        functools.partial(_mix_kernel, alpha=alpha),
        grid=(t // tm,),
        in_specs=[
            pl.BlockSpec((tm, D_MODEL), lambda i: (i, 0)),
            pl.BlockSpec((tm, RET_V_W), lambda i: (i, 0)),
            pl.BlockSpec((tm, FOX_W), lambda i: (i, 0)),
            pl.BlockSpec((tm, D_MODEL), lambda i: (i, OFF_GR // D_MODEL)),
            pl.BlockSpec((tm, D_MODEL), lambda i: (i, OFF_GF // D_MODEL)),
            pl.BlockSpec((tm, PLE_DIM), lambda i: (i, 0)),
            full((1, D_MODEL)), full((1, D_MODEL)), full((2, D_MODEL)),
            full((RET_V_W, D_MODEL)), full((FOX_W, D_MODEL)), full((D_MODEL, D_MODEL)),
            full((1, D_MODEL)), full((1, D_MODEL)),
            full((D_MODEL, D_MODEL)), full((1, D_MODEL)), full((PLE_DIM, D_MODEL)),
        ],
        out_specs=[
            pl.BlockSpec((tm, D_MODEL), lambda i: (i, 0)),
            pl.BlockSpec((1, D_MODEL, tm), lambda i: (i, 0, 0)),
        ],
        out_shape=[jax.ShapeDtypeStruct((t, D_MODEL), F32),
                   jax.ShapeDtypeStruct((t // tm, D_MODEL, tm), BF16)],
        compiler_params=_params("parallel"),
        name="mix",
    )(x2, yr, yf, proj, proj, p2, ge, be, bg, wro, wfo, wout, g1, b1, wpg, bpg, wple)


def _top16(sc):
    nk = sc.shape[0]
    iota = lax.broadcasted_iota(jnp.int32, sc.shape, 0).astype(F32)
    rank = jnp.full(sc.shape, float(PEER_TOPK), F32)
    vals = []
    for a in range(PEER_TOPK):
        m = jnp.max(sc, axis=0, keepdims=True)
        first = jnp.min(jnp.where(sc == m, iota, float(nk)), axis=0, keepdims=True)
        hit = iota == first
        sc = jnp.where(hit, NEG_BIG, sc)
        rank = jnp.where(hit, float(a), rank)
        vals.append(m)
    return vals, rank


def _mark(a):
    return MARK_BASE * (1.0 + (a + 1) / 64.0)


def _top16_no_ties(chains):
    vals = [[] for _ in chains]
    for a in range(PEER_TOPK):
        ms = [jnp.max(sc, axis=0, keepdims=True) for sc in chains]
        chains = [jnp.where(sc == m, _mark(a), sc) for sc, m in zip(chains, ms)]
        for v, m in zip(vals, ms):
            v.append(m)
    return vals, chains


def _route_kernel(ht_ref, wq_ref, keys_ref, ct_ref, w0_ref, r1_ref, w1_ref,
                  q_scr, sc_scr, rank_scr, val_scr, cnt_scr, z_scr, *, lanes):
    tb = ht_ref.shape[2]
    groups = [slice(g * lanes, (g + 1) * lanes) for g in range(tb // lanes)]
    q_scr[...] = jnp.dot(wq_ref[...], ht_ref[0], preferred_element_type=F32).astype(BF16)
    val_scr[...] = jnp.zeros_like(val_scr)

    for hc in range(2 * PEER_HEADS):
        sc_scr[hc] = jnp.dot(keys_ref[hc], q_scr[hc * PEER_HALF:(hc + 1) * PEER_HALF, :],
                             preferred_element_type=F32)

    def store_vals(c, h, sl, vals):
        own_row = lax.broadcasted_iota(jnp.int32, (PEER_HEADS, lanes), 0) == h
        for a in range(PEER_TOPK):
            val_scr[c, a, :, sl] = jnp.where(own_row, vals[a], val_scr[c, a, :, sl])

    def select_cells(sl, exact):
        flat = [float(a * PEER_TOPK + b) for a, b in CAND_CELLS]
        v0 = [val_scr[0, a, :, sl] for a in range(PEER_TOPK)]
        v1 = [val_scr[1, b, :, sl] for b in range(PEER_TOPK)]
        cand = [v0[a] + v1[b] for a, b in CAND_CELLS]
        if exact:
            sel = [jnp.zeros_like(v0[0]) for _ in CAND_CELLS]
            for _ in range(PEER_TOPK):
                m = functools.reduce(jnp.maximum, cand)
                first = functools.reduce(
                    jnp.minimum, [jnp.where(cv == m, fi, 1e9) for cv, fi in zip(cand, flat)])
                hits = [first == fi for fi in flat]
                cand = [jnp.where(hit, NEG_BIG, cv) for hit, cv in zip(hits, cand)]
                sel = [jnp.where(hit, 1.0, sv) for hit, sv in zip(hits, sel)]
        else:
            thr = functools.reduce(jnp.maximum, cand)
            for _ in range(PEER_TOPK - 1):
                thr = functools.reduce(jnp.maximum, [jnp.where(cv < thr, cv, NEG_BIG) for cv in cand])
            sel = [jnp.where(cv >= thr, 1.0, 0.0) for cv in cand]
        e0 = [jnp.exp(v0[a] - v0[0]) for a in range(PEER_TOPK)]
        e1 = [jnp.exp(v1[b] - v1[0]) for b in range(PEER_TOPK)]
        z = jnp.zeros_like(v0[0])
        cnt = [jnp.zeros_like(v0[0]) for _ in range(PEER_TOPK)]
        for t, (a, b) in enumerate(CAND_CELLS):
            z = z + sel[t] * (e0[a] * e1[b])
            cnt[a] = cnt[a] + sel[t]
        z_scr[:, sl] = 1.0 / z
        for a in range(PEER_TOPK):
            cnt_scr[a, :, sl] = cnt[a]
        return jnp.abs(functools.reduce(jnp.add, sel) - float(PEER_TOPK))

    def per_head(h, carry):
        bad = jnp.zeros((1, lanes), F32)
        for sl in groups:
            vals, marked = _top16_no_ties([sc_scr[2 * h, :, sl], sc_scr[2 * h + 1, :, sl]])
            for c in range(2):
                taken = marked[c] <= MARK_BASE
                unit = marked[c] * -(2.0 ** -100) * (2.0 ** -27)
                rank_scr[2 * h + c, :, sl] = jnp.where(taken, (unit - 1.0) * 64.0 - 1.0, float(PEER_TOPK))
                n_taken = jnp.sum(jnp.where(taken, 1.0, 0.0), axis=0, keepdims=True)
                bad = jnp.maximum(bad, jnp.abs(n_taken - float(PEER_TOPK)))
                store_vals(c, h, sl, vals[c])

        @pl.when(jnp.max(bad) > 0.0)
        def _():
            for c in range(2):
                for sl in groups:
                    vals, rank = _top16(sc_scr[2 * h + c, :, sl])
                    rank_scr[2 * h + c, :, sl] = rank
                    store_vals(c, h, sl, vals)

        return carry

    lax.fori_loop(0, PEER_HEADS, per_head, 0)

    for sl in groups:
        excess = select_cells(sl, False)

        @pl.when(jnp.max(excess) > 0.0)
        def _(sl=sl):
            select_cells(sl, True)

    for h in range(PEER_HEADS):
        s0 = sc_scr[2 * h]
        s1 = sc_scr[2 * h + 1]
        rank0 = rank_scr[2 * h]
        ct = jnp.zeros_like(s0)
        for a in range(PEER_TOPK):
            ct = jnp.where(rank0 == float(a), cnt_scr[a, h:h + 1, :], ct)
        ct_ref[0, h] = ct
        w0_ref[0, h] = jnp.exp(s0 - val_scr[0, 0, h:h + 1, :]) * (0.5 * z_scr[h:h + 1, :])
        w1_ref[0, h] = jnp.exp(s1 - val_scr[1, 0, h:h + 1, :]).astype(BF16)
        r1_ref[0, h] = rank_scr[2 * h + 1].astype(BF16)


def _route(ht, wq_t, keys, tb=256, lanes=128):
    tm = ht.shape[2]
    t = ht.shape[0] * tm
    per = tm // tb
    nq = 2 * PEER_HEADS * PEER_HALF
    routed = lambda dt: jax.ShapeDtypeStruct((t // tb, PEER_HEADS, PEER_NKEYS, tb), dt)
    rspec = pl.BlockSpec((1, PEER_HEADS, PEER_NKEYS, tb), lambda n: (n, 0, 0, 0))
    return pl.pallas_call(
        functools.partial(_route_kernel, lanes=lanes),
        grid=(t // tb,),
        in_specs=[
            pl.BlockSpec((1, D_MODEL, tb), lambda n: (n // per, 0, n % per)),
            pl.BlockSpec((nq, D_MODEL), lambda n: (0, 0)),
            pl.BlockSpec((2 * PEER_HEADS, PEER_NKEYS, PEER_HALF), lambda n: (0, 0, 0)),
        ],
        out_specs=[rspec, rspec, rspec, rspec],
        out_shape=[routed(F32), routed(F32), routed(BF16), routed(BF16)],
        scratch_shapes=[
            pltpu.VMEM((nq, tb), BF16),
            pltpu.VMEM((2 * PEER_HEADS, PEER_NKEYS, tb), F32),
            pltpu.VMEM((2 * PEER_HEADS, PEER_NKEYS, tb), F32),
            pltpu.VMEM((2, PEER_TOPK, PEER_HEADS, tb), F32),
            pltpu.VMEM((PEER_TOPK, PEER_HEADS, tb), F32),
            pltpu.VMEM((PEER_HEADS, tb), F32),
        ],
        compiler_params=_params("parallel"),
        name="route",
    )(ht, wq_t, keys)


def _peer_kernel(ht_ref, u_ref, vt_ref, ct_ref, w0_ref, r1_ref, w1_ref, base_ref, g2_ref, b2_ref,
                 o_ref, hid_scr, a_scr, acc_scr, r1_scr, w1_scr, *, eb, lanes):
    e = pl.program_id(1)
    tb = ht_ref.shape[2]
    tr = r1_ref.shape[3]
    sub = 16

    @pl.when(e == 0)
    def _():
        acc_scr[...] = jnp.zeros_like(acc_scr)
        for rb in range(tb // tr):
            r1_scr[:, :, rb * tr:(rb + 1) * tr] = r1_ref[rb]
            w1_scr[:, :, lanes + rb * tr:lanes + (rb + 1) * tr] = w1_ref[rb]

    hid_scr[...] = jnp.dot(u_ref[...], ht_ref[0], preferred_element_type=F32)
    for il in range(eb // PEER_NKEYS):
        for lg in range(tb // lanes):
            sl = slice(lg * lanes, (lg + 1) * lanes)
            sw = slice((lg + 1) * lanes, (lg + 2) * lanes)
            rb, rl = divmod(lg * lanes, tr)
            st = slice(rl, rl + lanes)
            cts = [jnp.broadcast_to(ct_ref[rb, h, il:il + 1, st], (sub, lanes)).astype(BF16)
                   for h in range(PEER_HEADS)]
            w0s = [jnp.broadcast_to(w0_ref[rb, h, il:il + 1, st], (sub, lanes)).astype(BF16)
                   for h in range(PEER_HEADS)]
            for jb in range(PEER_NKEYS // sub):
                js = slice(jb * sub, (jb + 1) * sub)
                gsum = None
                for h in range(PEER_HEADS):
                    gate0 = jnp.minimum(jnp.maximum(cts[h] - r1_scr[h, js, sl], 0.0), w0s[h])
                    term = gate0 * w1_scr[h, js, sw]
                    gsum = term if gsum is None else gsum + term
                rows = slice(il * PEER_NKEYS + jb * sub, il * PEER_NKEYS + (jb + 1) * sub)
                hv = hid_scr[rows, sl].astype(BF16)
                act = hv * (1.0 + lax.erf(hv * jnp.asarray(1.0 / math.sqrt(2.0), BF16)))
                a_scr[rows, sl] = act * gsum
    acc_scr[...] += jnp.dot(vt_ref[...], a_scr[...], preferred_element_type=F32)

    @pl.when(e == pl.num_programs(1) - 1)
    def _():
        o_ref[...] = _layer_norm(base_ref[...] + acc_scr[...].T, g2_ref[...], b2_ref[...])


def _peer(ht, u_bf, vt_bf, ct, w0, r1, w1, base, g2, b2, eb=2048, lanes=128):
    tb = ht.shape[2]
    t = ht.shape[0] * tb
    kr = eb // PEER_NKEYS
    tr = r1.shape[3]
    rspec = pl.BlockSpec((tb // tr, PEER_HEADS, PEER_NKEYS, tr), lambda n, e: (n, 0, 0, 0))
    kspec = pl.BlockSpec((tb // tr, PEER_HEADS, kr, tr), lambda n, e: (n, 0, e, 0))
    return pl.pallas_call(
        functools.partial(_peer_kernel, eb=eb, lanes=lanes),
        grid=(t // tb, PEER_NEXP // eb),
        in_specs=[
            pl.BlockSpec((1, D_MODEL, tb), lambda n, e: (n, 0, 0)),
            pl.BlockSpec((eb, D_MODEL), lambda n, e: (e, 0)),
            pl.BlockSpec((D_MODEL, eb), lambda n, e: (0, e)),
            kspec, kspec, rspec, rspec,
            pl.BlockSpec((tb, D_MODEL), lambda n, e: (n, 0)),
            pl.BlockSpec((1, D_MODEL), lambda n, e: (0, 0)),
            pl.BlockSpec((1, D_MODEL), lambda n, e: (0, 0)),
        ],
        out_specs=pl.BlockSpec((tb, D_MODEL), lambda n, e: (n, 0)),
        out_shape=jax.ShapeDtypeStruct((t, D_MODEL), F32),
        scratch_shapes=[
            pltpu.VMEM((eb, tb), F32),
            pltpu.VMEM((eb, tb), BF16),
            pltpu.VMEM((D_MODEL, tb), F32),
            pltpu.VMEM((PEER_HEADS, PEER_NKEYS, tb + lanes), BF16),
            pltpu.VMEM((PEER_HEADS, PEER_NKEYS, tb + lanes), BF16),
        ],
        compiler_params=_params("parallel", "arbitrary"),
        name="peer",
    )(ht, u_bf, vt_bf, ct, w0, r1, w1, base, g2, b2)


def _retention_tables(seq):
    half = RET_DK // 2
    inv = ROPE_BASE ** (-jnp.arange(half, dtype=F32) / half)
    ang = jnp.arange(seq, dtype=jnp.int32).astype(F32)[:, None] * inv[None, :]
    cos, sin = jnp.cos(ang), jnp.sin(ang)
    cos_t = jnp.concatenate([cos, cos], axis=-1)
    sin_t = jnp.concatenate([-sin, sin], axis=-1)
    c = RET_CHUNK
    log_g = jnp.log(1.0 - 2.0 ** (-5.0 - jnp.arange(RET_HEADS, dtype=F32)))
    idx = jnp.arange(c, dtype=F32)
    diff = idx[:, None] - idx[None, :]
    intra = jnp.where(diff >= 0, jnp.exp(log_g[:, None, None] * jnp.maximum(diff, 0.0)), 0.0)
    qdec = jnp.broadcast_to(jnp.exp(log_g[:, None] * (idx + 1.0))[..., None], (RET_HEADS, c, RET_DK))
    kdec = jnp.broadcast_to(jnp.exp(log_g[:, None] * (c - 1.0 - idx))[..., None], (RET_HEADS, c, RET_DK))
    chunk_decay = tuple(math.exp(math.log(1.0 - 2.0 ** (-5.0 - h)) * c) for h in range(RET_HEADS))
    return cos_t, sin_t, intra, qdec, kdec, chunk_decay


def _pack_w_in(w):
    rq, rk, rv, rg, fq, fk, fv, ff, gr, gf = jnp.split(
        w, [512, 1024, 2048, 3072, 3584, 4096, 4608, 4616, 5640], axis=-1)
    main = jnp.concatenate([rq, rk, rv, rg, gr, gf, fq, fk, fv], axis=-1).astype(BF16)
    ff_pad = jnp.zeros((w.shape[0], FF_IN), w.dtype).at[:, 0:FOX_HEADS].set(ff).astype(BF16)
    return main, ff_pad


def kernel(x, p, ln_emb_g, ln_emb_b, w_in, b_forget, b_branch_gate, w_ret_o, w_fox_o, w_out, ln1_g, ln1_b,
           w_peer_q, peer_sub_keys, peer_u, peer_v, w_ple_gate, b_ple_gate, w_ple, ln2_g, ln2_b):
    batch, seq, d = x.shape
    depth = w_in.shape[0]
    assert depth == 1 and d == D_MODEL, "the trunk-entry norm is fused into the single layer"
    t = batch * seq
    alpha = (2.0 * depth) ** 0.25
    fox_tq = min(512, seq)
    row = lambda v: v.reshape(1, -1).astype(F32)
    cos_t, sin_t, intra, qdec, kdec, chunk_decay = _retention_tables(seq)

    x2 = x.reshape(t, d)
    ge, be = row(ln_emb_g), row(ln_emb_b)
    w_main, w_ff = _pack_w_in(w_in[0])
    bf_pad = jnp.zeros((1, FF_IN), F32).at[0, 0:FOX_HEADS].set(b_forget[0].astype(F32))
    proj, ff = _in_proj(x2, ge, be, w_main, w_ff)
    yr = _retention(proj, cos_t, sin_t, intra, qdec, kdec, chunk_decay, batch, seq)
    c, ct = _fcumsum(ff, bf_pad, batch, seq, fox_tq)
    yf = _fox(proj, c, ct, batch, seq, fox_tq)
    base, ht = _mix(x2, yr, yf, proj, p[0].reshape(t, PLE_DIM), ge, be, b_branch_gate[0].astype(F32),
                    w_ret_o[0].astype(BF16), w_fox_o[0].astype(BF16), w_out[0].astype(BF16),
                    row(ln1_g[0]), row(ln1_b[0]), w_ple_gate[0].astype(BF16), row(b_ple_gate[0]),
                    w_ple[0].astype(BF16), alpha)
    keys = peer_sub_keys[0].reshape(2 * PEER_HEADS, PEER_NKEYS, PEER_HALF).astype(BF16)
    ct_r, w0, r1, w1 = _route(ht, w_peer_q[0].T.astype(BF16), keys)
    out = _peer(ht, peer_u[0].astype(BF16), peer_v[0].T.astype(BF16), ct_r, w0, r1, w1, base,
                row(ln2_g[0]), row(ln2_b[0]))
    return out.reshape(batch, seq, d)
```

```python
import functools
import math

import jax
import jax.numpy as jnp
from jax import lax
from jax.experimental import pallas as pl
from jax.experimental.pallas import tpu as pltpu

F32 = jnp.float32
BF16 = jnp.bfloat16

D_MODEL = 1024
RET_HEADS = 4
RET_DK = 128
RET_DV = 256
RET_CHUNK = 128
FOX_HEADS = 8
FOX_DH = 64
PEER_HEADS = 8
PEER_NKEYS = 128
PEER_NEXP = PEER_NKEYS * PEER_NKEYS
PEER_HALF = 128
PEER_TOPK = 16
PLE_DIM = 256
LN_EPS = 1e-5
ROPE_BASE = 10000.0

RET_QK_W = RET_HEADS * RET_DK
RET_V_W = RET_HEADS * RET_DV
FOX_W = FOX_HEADS * FOX_DH

OFF_RQ, OFF_RK, OFF_RV, OFF_RG = 0, 512, 1024, 2048
OFF_GR, OFF_GF = 3072, 4096
OFF_FQ, OFF_FK, OFF_FV = 5120, 5632, 6144
PROJ_W = 6656
FF_IN = 128
FF_W = 512

V7X_VMEM_LIMIT = 56 * 1024 * 1024
NEG_BIG = -3.0e38
MARK_BASE = -(2.0 ** 127)
CAND_CELLS = tuple((a, b) for a in range(PEER_TOPK) for b in range(PEER_TOPK)
                   if (a + 1) * (b + 1) <= PEER_TOPK)


def _layer_norm(x, g, b):
    mu = jnp.mean(x, axis=-1, keepdims=True)
    xc = x - mu
    var = jnp.mean(xc * xc, axis=-1, keepdims=True)
    return xc * lax.rsqrt(var + LN_EPS) * g + b


def _params(*sem):
    return pltpu.CompilerParams(dimension_semantics=sem, vmem_limit_bytes=V7X_VMEM_LIMIT)


def _inproj_kernel(x_ref, g_ref, b_ref, w_ref, wff_ref, o_ref, ff_ref, h_scr):
    @pl.when(pl.program_id(1) == 0)
    def _():
        hb = _layer_norm(x_ref[...], g_ref[...], b_ref[...]).astype(BF16)
        h_scr[...] = hb
        ff_ref[...] = jnp.dot(hb, wff_ref[...], preferred_element_type=F32)

    o_ref[...] = jnp.dot(h_scr[...], w_ref[...], preferred_element_type=F32).astype(BF16)


def _in_proj(x2, g, b, w_main, w_ff, tm=1024, tn=3328):
    t = x2.shape[0]
    return pl.pallas_call(
        _inproj_kernel,
        grid=(t // tm, PROJ_W // tn),
        in_specs=[
            pl.BlockSpec((tm, D_MODEL), lambda i, j: (i, 0)),
            pl.BlockSpec((1, D_MODEL), lambda i, j: (0, 0)),
            pl.BlockSpec((1, D_MODEL), lambda i, j: (0, 0)),
            pl.BlockSpec((D_MODEL, tn), lambda i, j: (0, j)),
            pl.BlockSpec((D_MODEL, FF_IN), lambda i, j: (0, 0)),
        ],
        out_specs=[
            pl.BlockSpec((tm, tn), lambda i, j: (i, j)),
            pl.BlockSpec((tm, FF_IN), lambda i, j: (i, 0)),
        ],
        out_shape=[jax.ShapeDtypeStruct((t, PROJ_W), BF16), jax.ShapeDtypeStruct((t, FF_IN), F32)],
        scratch_shapes=[pltpu.VMEM((tm, D_MODEL), BF16)],
        compiler_params=_params("parallel", "arbitrary"),
        name="in_proj",
    )(x2, g, b, w_main, w_ff)


def _retention_kernel(q_ref, k_ref, v_ref, g_ref, cos_ref, sin_ref, intra_ref, qd_ref, kd_ref,
                      o_ref, r_scr, *, chunk_decay):
    @pl.when(pl.program_id(1) == 0)
    def _():
        r_scr[...] = jnp.zeros_like(r_scr)

    for ci in range(q_ref.shape[0] // RET_CHUNK):
        rows = slice(ci * RET_CHUNK, (ci + 1) * RET_CHUNK)
        cos = cos_ref[rows, :]
        sin = sin_ref[rows, :]
        for h in range(RET_HEADS):
            q = q_ref[rows, h * RET_DK:(h + 1) * RET_DK].astype(F32)
            k = k_ref[rows, h * RET_DK:(h + 1) * RET_DK].astype(F32)
            v = v_ref[rows, h * RET_DV:(h + 1) * RET_DV]
            qr = q * cos + pltpu.roll(q, RET_DK // 2, 1) * sin
            kr = (k * cos + pltpu.roll(k, RET_DK // 2, 1) * sin) * (RET_DK ** -0.5)
            s = lax.dot_general(qr.astype(BF16), kr.astype(BF16), (((1,), (1,)), ((), ())),
                                preferred_element_type=F32) * intra_ref[h]
            inner = jnp.dot(s.astype(BF16), v, preferred_element_type=F32)
            r_old = r_scr[h]
            cross = jnp.dot((qr * qd_ref[h]).astype(BF16), r_old.astype(BF16), preferred_element_type=F32)
            kv = lax.dot_general((kr * kd_ref[h]).astype(BF16), v, (((0,), (0,)), ((), ())),
                                 preferred_element_type=F32)
            r_scr[h] = chunk_decay[h] * r_old + kv
            y = inner + cross
            mu = jnp.mean(y, axis=-1, keepdims=True)
            yc = y - mu
            var = jnp.mean(yc * yc, axis=-1, keepdims=True)
            yn = yc * lax.rsqrt(var + LN_EPS)
            gate = g_ref[rows, h * RET_DV:(h + 1) * RET_DV].astype(F32)
            o_ref[rows, h * RET_DV:(h + 1) * RET_DV] = (gate * jax.nn.sigmoid(gate) * yn).astype(BF16)


def _retention(proj, cos_t, sin_t, intra, qdec, kdec, chunk_decay, batch, seq, chunks_per_step=8):
    c = RET_CHUNK * chunks_per_step
    nc = seq // c
    row = lambda b, j: b * nc + j
    return pl.pallas_call(
        functools.partial(_retention_kernel, chunk_decay=chunk_decay),
        grid=(batch, nc),
        in_specs=[
            pl.BlockSpec((c, RET_QK_W), lambda b, j: (row(b, j), OFF_RQ // RET_QK_W)),
            pl.BlockSpec((c, RET_QK_W), lambda b, j: (row(b, j), OFF_RK // RET_QK_W)),
            pl.BlockSpec((c, RET_V_W), lambda b, j: (row(b, j), OFF_RV // RET_V_W)),
            pl.BlockSpec((c, RET_V_W), lambda b, j: (row(b, j), OFF_RG // RET_V_W)),
            pl.BlockSpec((c, RET_DK), lambda b, j: (j, 0)),
            pl.BlockSpec((c, RET_DK), lambda b, j: (j, 0)),
            pl.BlockSpec((RET_HEADS, RET_CHUNK, RET_CHUNK), lambda b, j: (0, 0, 0)),
            pl.BlockSpec((RET_HEADS, RET_CHUNK, RET_DK), lambda b, j: (0, 0, 0)),
            pl.BlockSpec((RET_HEADS, RET_CHUNK, RET_DK), lambda b, j: (0, 0, 0)),
        ],
        out_specs=pl.BlockSpec((c, RET_V_W), lambda b, j: (row(b, j), 0)),
        out_shape=jax.ShapeDtypeStruct((batch * seq, RET_V_W), BF16),
        scratch_shapes=[pltpu.VMEM((RET_HEADS, RET_DK, RET_DV), F32)],
        compiler_params=_params("parallel", "arbitrary"),
        name="retention",
    )(proj, proj, proj, proj, cos_t, sin_t, intra, qdec, kdec)


def _fcumsum_kernel(ff_ref, bf_ref, tri_ref, c_ref, ct_ref, *, blk):
    seq = ff_ref.shape[0]
    carry = jnp.zeros((1, FF_IN), F32)
    for i in range(seq // blk):
        lf = jax.nn.log_sigmoid(ff_ref[i * blk:(i + 1) * blk, :] + bf_ref[...])
        cs = jnp.dot(tri_ref[...], lf, preferred_element_type=F32,
                     precision=lax.Precision.HIGHEST) + carry
        carry = cs[blk - 1:blk, :]
        for p in range(FOX_HEADS // 2):
            pair = cs if p == 0 else pltpu.roll(cs, FF_IN - 2 * p, 1)
            c_ref[i * blk:(i + 1) * blk, p * 128:(p + 1) * 128] = pair
            kb, off = divmod(i * blk, ct_ref.shape[4])
            ct_ref[0, p, kb, :, off:off + blk] = pair.T[0:8, :]


def _fcumsum(ff, bf_pad, batch, seq, tq, blk=512):
    blk = min(blk, tq)
    tri = (lax.broadcasted_iota(jnp.int32, (blk, blk), 0)
           >= lax.broadcasted_iota(jnp.int32, (blk, blk), 1)).astype(F32)
    nb = seq // tq
    return pl.pallas_call(
        functools.partial(_fcumsum_kernel, blk=blk),
        grid=(batch,),
        in_specs=[
            pl.BlockSpec((seq, FF_IN), lambda b: (b, 0)),
            pl.BlockSpec((1, FF_IN), lambda b: (0, 0)),
            pl.BlockSpec((blk, blk), lambda b: (0, 0)),
        ],
        out_specs=[
            pl.BlockSpec((seq, FF_W), lambda b: (b, 0)),
            pl.BlockSpec((1, FOX_HEADS // 2, nb, 8, tq), lambda b: (b, 0, 0, 0, 0)),
        ],
        out_shape=[jax.ShapeDtypeStruct((batch * seq, FF_W), F32),
                   jax.ShapeDtypeStruct((batch, FOX_HEADS // 2, nb, 8, tq), F32)],
        compiler_params=_params("parallel"),
        name="fcumsum",
    )(ff, bf_pad, tri)


def _fox_kernel(q_ref, k_ref, v_ref, c_ref, ct_ref, o_ref, vt_scr, ckb_scr, m_scr, l_scr, acc_scr, *, tq):
    qi = pl.program_id(2)
    seq = k_ref.shape[0]
    nkv = seq // tq

    @pl.when(qi == 0)
    def _():
        for j in range(nkv):
            vt_scr[j] = v_ref[j * tq:(j + 1) * tq, :].astype(F32).T.astype(BF16)
        for r in range(2):
            ckb_scr[r] = jnp.broadcast_to(c_ref[:, r:r + 1], (seq, 128))

    lane = lax.broadcasted_iota(jnp.int32, (tq, 128), 1)
    key_i = lax.broadcasted_iota(jnp.int32, (tq, tq), 0)
    qry_i = lax.broadcasted_iota(jnp.int32, (tq, tq), 1)
    q = q_ref[...] * jnp.asarray(FOX_DH ** -0.5, BF16)
    qh = [jnp.where((lane >= r * FOX_DH) & (lane < (r + 1) * FOX_DH), q, jnp.zeros_like(q)) for r in range(2)]
    cq = [ct_ref[0, 0, qi, r:r + 1, :] for r in range(2)]
    m_scr[...] = jnp.full_like(m_scr, NEG_BIG)
    l_scr[...] = jnp.zeros_like(l_scr)
    acc_scr[...] = jnp.zeros_like(acc_scr)

    def block(j, diag):
        start = pl.multiple_of(j * tq, tq)
        kb = k_ref[pl.ds(start, tq), :]
        vtb = vt_scr[j]
        s = [lax.dot_general(kb, qh[r], (((1,), (1,)), ((), ())), preferred_element_type=F32)
             for r in range(2)]
        ck = [ckb_scr[r, pl.ds(start, tq), :] for r in range(2)]
        s = [s[r] - jnp.concatenate([ck[r]] * (tq // 128), axis=1) for r in range(2)]
        if diag:
            s = [jnp.where(key_i <= qry_i, s[r], -1e30) for r in range(2)]
        m_old = [m_scr[r] for r in range(2)]
        m_new = [jnp.maximum(m_old[r], jnp.max(s[r], axis=0, keepdims=True) + cq[r]) for r in range(2)]
        alpha = [jnp.exp(m_old[r] - m_new[r]) for r in range(2)]
        p = [jnp.exp(s[r] + (cq[r] - m_new[r])) for r in range(2)]
        for r in range(2):
            l_scr[r] = alpha[r] * l_scr[r] + jnp.sum(p[r], axis=0, keepdims=True)
            pv = jnp.dot(vtb[r * FOX_DH:(r + 1) * FOX_DH, :], p[r].astype(BF16), preferred_element_type=F32)
            acc_scr[r] = alpha[r] * acc_scr[r] + pv
            m_scr[r] = m_new[r]

    def body(j, carry):
        block(j, False)
        return carry

    lax.fori_loop(0, qi, body, 0)
    block(qi, True)
    out_t = jnp.concatenate([acc_scr[r] / l_scr[r] for r in range(2)], axis=0)
    o_ref[...] = out_t.T.astype(BF16)


def _fox(proj, c, ct, batch, seq, tq):
    nq = seq // tq
    np_ = FOX_HEADS // 2
    return pl.pallas_call(
        functools.partial(_fox_kernel, tq=tq),
        grid=(batch, np_, nq),
        in_specs=[
            pl.BlockSpec((tq, 128), lambda b, p, i: (b * nq + i, OFF_FQ // 128 + p)),
            pl.BlockSpec((seq, 128), lambda b, p, i: (b, OFF_FK // 128 + p)),
            pl.BlockSpec((seq, 128), lambda b, p, i: (b, OFF_FV // 128 + p)),
            pl.BlockSpec((seq, 128), lambda b, p, i: (b, p)),
            pl.BlockSpec((1, 1, nq, 8, tq), lambda b, p, i: (b, p, 0, 0, 0)),
        ],
        out_specs=pl.BlockSpec((tq, 128), lambda b, p, i: (b * nq + i, p)),
        out_shape=jax.ShapeDtypeStruct((batch * seq, FOX_W), BF16),
        scratch_shapes=[
            pltpu.VMEM((nq, 128, tq), BF16),
            pltpu.VMEM((2, seq, 128), F32),
            pltpu.VMEM((2, 1, tq), F32),
            pltpu.VMEM((2, 1, tq), F32),
            pltpu.VMEM((2, FOX_DH, tq), F32),
        ],
        compiler_params=_params("parallel", "parallel", "arbitrary"),
        name="fox",
    )(proj, proj, proj, c, ct)


def _mix_kernel(x_ref, yr_ref, yf_ref, gr_ref, gf_ref, p_ref, ge_ref, be_ref, bg_ref, wro_ref, wfo_ref,
                wout_ref, g1_ref, b1_ref, wpg_ref, bpg_ref, wple_ref, base_ref, ht_ref, *, alpha):
    h0 = _layer_norm(x_ref[...], ge_ref[...], be_ref[...])
    y_ret = jnp.dot(yr_ref[...], wro_ref[...], preferred_element_type=F32)
    y_fox = jnp.dot(yf_ref[...], wfo_ref[...], preferred_element_type=F32)
    merged = (jax.nn.sigmoid(gr_ref[...].astype(F32) + bg_ref[0:1, :]) * y_ret
              + jax.nn.sigmoid(gf_ref[...].astype(F32) + bg_ref[1:2, :]) * y_fox)
    out = jnp.dot(merged.astype(BF16), wout_ref[...], preferred_element_type=F32)
    h1 = _layer_norm(alpha * h0 + out, g1_ref[...], b1_ref[...])
    h1b = h1.astype(BF16)
    gate = jax.nn.sigmoid(jnp.dot(h1b, wpg_ref[...], preferred_element_type=F32) + bpg_ref[...])
    ple = gate * jnp.dot(p_ref[...].astype(BF16), wple_ref[...], preferred_element_type=F32)
    base_ref[...] = alpha * h1 + ple
    ht_ref[0] = h1.T.astype(BF16)


def _mix(x2, yr, yf, proj, p2, ge, be, bg, wro, wfo, wout, g1, b1, wpg, bpg, wple, alpha, tm=512):
    t = x2.shape[0]
    full = lambda shape: pl.BlockSpec(shape, lambda i: (0,) * len(shape))
    return pl.pallas_call(
        functools.partial(_mix_kernel, alpha=alpha),
        grid=(t // tm,),
        in_specs=[
            pl.BlockSpec((tm, D_MODEL), lambda i: (i, 0)),
            pl.BlockSpec((tm, RET_V_W), lambda i: (i, 0)),
            pl.BlockSpec((tm, FOX_W), lambda i: (i, 0)),
            pl.BlockSpec((tm, D_MODEL), lambda i: (i, OFF_GR // D_MODEL)),
            pl.BlockSpec((tm, D_MODEL), lambda i: (i, OFF_GF // D_MODEL)),
            pl.BlockSpec((tm, PLE_DIM), lambda i: (i, 0)),
            full((1, D_MODEL)), full((1, D_MODEL)), full((2, D_MODEL)),
            full((RET_V_W, D_MODEL)), full((FOX_W, D_MODEL)), full((D_MODEL, D_MODEL)),
            full((1, D_MODEL)), full((1, D_MODEL)),
            full((D_MODEL, D_MODEL)), full((1, D_MODEL)), full((PLE_DIM, D_MODEL)),
        ],
        out_specs=[
            pl.BlockSpec((tm, D_MODEL), lambda i: (i, 0)),
            pl.BlockSpec((1, D_MODEL, tm), lambda i: (i, 0, 0)),
        ],
        out_shape=[jax.ShapeDtypeStruct((t, D_MODEL), F32),
                   jax.ShapeDtypeStruct((t // tm, D_MODEL, tm), BF16)],
        compiler_params=_params("parallel"),
        name="mix",
    )(x2, yr, yf, proj, proj, p2, ge, be, bg, wro, wfo, wout, g1, b1, wpg, bpg, wple)


def _top16(sc):
    nk = sc.shape[0]
    iota = lax.broadcasted_iota(jnp.int32, sc.shape, 0).astype(F32)
    rank = jnp.full(sc.shape, float(PEER_TOPK), F32)
    vals = []
    for a in range(PEER_TOPK):
        m = jnp.max(sc, axis=0, keepdims=True)
        first = jnp.min(jnp.where(sc == m, iota, float(nk)), axis=0, keepdims=True)
        hit = iota == first
        sc = jnp.where(hit, NEG_BIG, sc)
        rank = jnp.where(hit, float(a), rank)
        vals.append(m)
    return vals, rank


def _mark(a):
    return MARK_BASE * (1.0 + (a + 1) / 64.0)


def _top16_no_ties(chains):
    vals = [[] for _ in chains]
    for a in range(PEER_TOPK):
        ms = [jnp.max(sc, axis=0, keepdims=True) for sc in chains]
        chains = [jnp.where(sc == m, _mark(a), sc) for sc, m in zip(chains, ms)]
        for v, m in zip(vals, ms):
            v.append(m)
    return vals, chains


def _route_kernel(ht_ref, wq_ref, keys_ref, ct_ref, w0_ref, r1_ref, w1_ref,
                  q_scr, sc_scr, rank_scr, val_scr, cnt_scr, z_scr, *, lanes):
    tb = ht_ref.shape[2]
    groups = [slice(g * lanes, (g + 1) * lanes) for g in range(tb // lanes)]
    q_scr[...] = jnp.dot(wq_ref[...], ht_ref[0], preferred_element_type=F32).astype(BF16)
    val_scr[...] = jnp.zeros_like(val_scr)

    for hc in range(2 * PEER_HEADS):
        sc_scr[hc] = jnp.dot(keys_ref[hc], q_scr[hc * PEER_HALF:(hc + 1) * PEER_HALF, :],
                             preferred_element_type=F32)

    def store_vals(c, h, sl, vals):
        own_row = lax.broadcasted_iota(jnp.int32, (PEER_HEADS, lanes), 0) == h
        for a in range(PEER_TOPK):
            val_scr[c, a, :, sl] = jnp.where(own_row, vals[a], val_scr[c, a, :, sl])

    def select_cells(sl, exact):
        flat = [float(a * PEER_TOPK + b) for a, b in CAND_CELLS]
        v0 = [val_scr[0, a, :, sl] for a in range(PEER_TOPK)]
        v1 = [val_scr[1, b, :, sl] for b in range(PEER_TOPK)]
        cand = [v0[a] + v1[b] for a, b in CAND_CELLS]
        if exact:
            sel = [jnp.zeros_like(v0[0]) for _ in CAND_CELLS]
            for _ in range(PEER_TOPK):
                m = functools.reduce(jnp.maximum, cand)
                first = functools.reduce(
                    jnp.minimum, [jnp.where(cv == m, fi, 1e9) for cv, fi in zip(cand, flat)])
                hits = [first == fi for fi in flat]
                cand = [jnp.where(hit, NEG_BIG, cv) for hit, cv in zip(hits, cand)]
                sel = [jnp.where(hit, 1.0, sv) for hit, sv in zip(hits, sel)]
        else:
            thr = functools.reduce(jnp.maximum, cand)
            for _ in range(PEER_TOPK - 1):
                thr = functools.reduce(jnp.maximum, [jnp.where(cv < thr, cv, NEG_BIG) for cv in cand])
            sel = [jnp.where(cv >= thr, 1.0, 0.0) for cv in cand]
        e0 = [jnp.exp(v0[a] - v0[0]) for a in range(PEER_TOPK)]
        e1 = [jnp.exp(v1[b] - v1[0]) for b in range(PEER_TOPK)]
        z = jnp.zeros_like(v0[0])
        cnt = [jnp.zeros_like(v0[0]) for _ in range(PEER_TOPK)]
        for t, (a, b) in enumerate(CAND_CELLS):
            z = z + sel[t] * (e0[a] * e1[b])
            cnt[a] = cnt[a] + sel[t]
        z_scr[:, sl] = 1.0 / z
        for a in range(PEER_TOPK):
            cnt_scr[a, :, sl] = cnt[a]
        return jnp.abs(functools.reduce(jnp.add, sel) - float(PEER_TOPK))

    def per_head(h, carry):
        bad = jnp.zeros((1, lanes), F32)
        for sl in groups:
            vals, marked = _top16_no_ties([sc_scr[2 * h, :, sl], sc_scr[2 * h + 1, :, sl]])
            for c in range(2):
                taken = marked[c] <= MARK_BASE
                unit = marked[c] * -(2.0 ** -100) * (2.0 ** -27)
                rank_scr[2 * h + c, :, sl] = jnp.where(taken, (unit - 1.0) * 64.0 - 1.0, float(PEER_TOPK))
                n_taken = jnp.sum(jnp.where(taken, 1.0, 0.0), axis=0, keepdims=True)
                bad = jnp.maximum(bad, jnp.abs(n_taken - float(PEER_TOPK)))
                store_vals(c, h, sl, vals[c])

        @pl.when(jnp.max(bad) > 0.0)
        def _():
            for c in range(2):
                for sl in groups:
                    vals, rank = _top16(sc_scr[2 * h + c, :, sl])
                    rank_scr[2 * h + c, :, sl] = rank
                    store_vals(c, h, sl, vals)

        return carry

    lax.fori_loop(0, PEER_HEADS, per_head, 0)

    for sl in groups:
        excess = select_cells(sl, False)

        @pl.when(jnp.max(excess) > 0.0)
        def _(sl=sl):
            select_cells(sl, True)

    for h in range(PEER_HEADS):
        s0 = sc_scr[2 * h]
        s1 = sc_scr[2 * h + 1]
        rank0 = rank_scr[2 * h]
        ct = jnp.zeros_like(s0)
        for a in range(PEER_TOPK):
            ct = jnp.where(rank0 == float(a), cnt_scr[a, h:h + 1, :], ct)
        ct_ref[0, h] = ct
        w0_ref[0, h] = jnp.exp(s0 - val_scr[0, 0, h:h + 1, :]) * (0.5 * z_scr[h:h + 1, :])
        w1_ref[0, h] = jnp.exp(s1 - val_scr[1, 0, h:h + 1, :]).astype(BF16)
        r1_ref[0, h] = rank_scr[2 * h + 1].astype(BF16)


def _route(ht, wq_t, keys, tb=256, lanes=128):
    tm = ht.shape[2]
    t = ht.shape[0] * tm
    per = tm // tb
    nq = 2 * PEER_HEADS * PEER_HALF
    routed = lambda dt: jax.ShapeDtypeStruct((t // tb, PEER_HEADS, PEER_NKEYS, tb), dt)
    rspec = pl.BlockSpec((1, PEER_HEADS, PEER_NKEYS, tb), lambda n: (n, 0, 0, 0))
    return pl.pallas_call(
        functools.partial(_route_kernel, lanes=lanes),
        grid=(t // tb,),
        in_specs=[
            pl.BlockSpec((1, D_MODEL, tb), lambda n: (n // per, 0, n % per)),
            pl.BlockSpec((nq, D_MODEL), lambda n: (0, 0)),
            pl.BlockSpec((2 * PEER_HEADS, PEER_NKEYS, PEER_HALF), lambda n: (0, 0, 0)),
        ],
        out_specs=[rspec, rspec, rspec, rspec],
        out_shape=[routed(F32), routed(F32), routed(BF16), routed(BF16)],
        scratch_shapes=[
            pltpu.VMEM((nq, tb), BF16),
            pltpu.VMEM((2 * PEER_HEADS, PEER_NKEYS, tb), F32),
            pltpu.VMEM((2 * PEER_HEADS, PEER_NKEYS, tb), F32),
            pltpu.VMEM((2, PEER_TOPK, PEER_HEADS, tb), F32),
            pltpu.VMEM((PEER_TOPK, PEER_HEADS, tb), F32),
            pltpu.VMEM((PEER_HEADS, tb), F32),
        ],
        compiler_params=_params("parallel"),
        name="route",
    )(ht, wq_t, keys)


def _peer_kernel(ht_ref, u_ref, vt_ref, ct_ref, w0_ref, r1_ref, w1_ref, base_ref, g2_ref, b2_ref,
                 o_ref, hid_scr, a_scr, acc_scr, r1_scr, w1_scr, *, eb, lanes):
    e = pl.program_id(1)
    tb = ht_ref.shape[2]
    tr = r1_ref.shape[3]
    sub = 16

    @pl.when(e == 0)
    def _():
        acc_scr[...] = jnp.zeros_like(acc_scr)
        for rb in range(tb // tr):
            r1_scr[:, :, rb * tr:(rb + 1) * tr] = r1_ref[rb]
            w1_scr[:, :, lanes + rb * tr:lanes + (rb + 1) * tr] = w1_ref[rb]

    hid_scr[...] = jnp.dot(u_ref[...], ht_ref[0], preferred_element_type=F32)
    for il in range(eb // PEER_NKEYS):
        for lg in range(tb // lanes):
            sl = slice(lg * lanes, (lg + 1) * lanes)
            sw = slice((lg + 1) * lanes, (lg + 2) * lanes)
            rb, rl = divmod(lg * lanes, tr)
            st = slice(rl, rl + lanes)
            cts = [jnp.broadcast_to(ct_ref[rb, h, il:il + 1, st], (sub, lanes)).astype(BF16)
                   for h in range(PEER_HEADS)]
            w0s = [jnp.broadcast_to(w0_ref[rb, h, il:il + 1, st], (sub, lanes)).astype(BF16)
                   for h in range(PEER_HEADS)]
            for jb in range(PEER_NKEYS // sub):
                js = slice(jb * sub, (jb + 1) * sub)
                gsum = None
                for h in range(PEER_HEADS):
                    gate0 = jnp.minimum(jnp.maximum(cts[h] - r1_scr[h, js, sl], 0.0), w0s[h])
                    term = gate0 * w1_scr[h, js, sw]
                    gsum = term if gsum is None else gsum + term
                rows = slice(il * PEER_NKEYS + jb * sub, il * PEER_NKEYS + (jb + 1) * sub)
                hv = hid_scr[rows, sl].astype(BF16)
                act = hv * (1.0 + lax.erf(hv * jnp.asarray(1.0 / math.sqrt(2.0), BF16)))
                a_scr[rows, sl] = act * gsum
    acc_scr[...] += jnp.dot(vt_ref[...], a_scr[...], preferred_element_type=F32)

    @pl.when(e == pl.num_programs(1) - 1)
    def _():
        o_ref[...] = _layer_norm(base_ref[...] + acc_scr[...].T, g2_ref[...], b2_ref[...])


def _peer(ht, u_bf, vt_bf, ct, w0, r1, w1, base, g2, b2, eb=2048, lanes=128):
    tb = ht.shape[2]
    t = ht.shape[0] * tb
    kr = eb // PEER_NKEYS
    tr = r1.shape[3]
    rspec = pl.BlockSpec((tb // tr, PEER_HEADS, PEER_NKEYS, tr), lambda n, e: (n, 0, 0, 0))
    kspec = pl.BlockSpec((tb // tr, PEER_HEADS, kr, tr), lambda n, e: (n, 0, e, 0))
    return pl.pallas_call(
        functools.partial(_peer_kernel, eb=eb, lanes=lanes),
        grid=(t // tb, PEER_NEXP // eb),
        in_specs=[
            pl.BlockSpec((1, D_MODEL, tb), lambda n, e: (n, 0, 0)),
            pl.BlockSpec((eb, D_MODEL), lambda n, e: (e, 0)),
            pl.BlockSpec((D_MODEL, eb), lambda n, e: (0, e)),
            kspec, kspec, rspec, rspec,
            pl.BlockSpec((tb, D_MODEL), lambda n, e: (n, 0)),
            pl.BlockSpec((1, D_MODEL), lambda n, e: (0, 0)),
            pl.BlockSpec((1, D_MODEL), lambda n, e: (0, 0)),
        ],
        out_specs=pl.BlockSpec((tb, D_MODEL), lambda n, e: (n, 0)),
        out_shape=jax.ShapeDtypeStruct((t, D_MODEL), F32),
        scratch_shapes=[
            pltpu.VMEM((eb, tb), F32),
            pltpu.VMEM((eb, tb), BF16),
            pltpu.VMEM((D_MODEL, tb), F32),
            pltpu.VMEM((PEER_HEADS, PEER_NKEYS, tb + lanes), BF16),
            pltpu.VMEM((PEER_HEADS, PEER_NKEYS, tb + lanes), BF16),
        ],
        compiler_params=_params("parallel", "arbitrary"),
        name="peer",
    )(ht, u_bf, vt_bf, ct, w0, r1, w1, base, g2, b2)


def _retention_tables(seq):
    half = RET_DK // 2
    inv = ROPE_BASE ** (-jnp.arange(half, dtype=F32) / half)
    ang = jnp.arange(seq, dtype=jnp.int32).astype(F32)[:, None] * inv[None, :]
    cos, sin = jnp.cos(ang), jnp.sin(ang)
    cos_t = jnp.concatenate([cos, cos], axis=-1)
    sin_t = jnp.concatenate([-sin, sin], axis=-1)
    c = RET_CHUNK
    log_g = jnp.log(1.0 - 2.0 ** (-5.0 - jnp.arange(RET_HEADS, dtype=F32)))
    idx = jnp.arange(c, dtype=F32)
    diff = idx[:, None] - idx[None, :]
    intra = jnp.where(diff >= 0, jnp.exp(log_g[:, None, None] * jnp.maximum(diff, 0.0)), 0.0)
    qdec = jnp.broadcast_to(jnp.exp(log_g[:, None] * (idx + 1.0))[..., None], (RET_HEADS, c, RET_DK))
    kdec = jnp.broadcast_to(jnp.exp(log_g[:, None] * (c - 1.0 - idx))[..., None], (RET_HEADS, c, RET_DK))
    chunk_decay = tuple(math.exp(math.log(1.0 - 2.0 ** (-5.0 - h)) * c) for h in range(RET_HEADS))
    return cos_t, sin_t, intra, qdec, kdec, chunk_decay


def _pack_w_in(w):
    rq, rk, rv, rg, fq, fk, fv, ff, gr, gf = jnp.split(
        w, [512, 1024, 2048, 3072, 3584, 4096, 4608, 4616, 5640], axis=-1)
    main = jnp.concatenate([rq, rk, rv, rg, gr, gf, fq, fk, fv], axis=-1).astype(BF16)
    ff_pad = jnp.zeros((w.shape[0], FF_IN), w.dtype).at[:, 0:FOX_HEADS].set(ff).astype(BF16)
    return main, ff_pad


def kernel(x, p, ln_emb_g, ln_emb_b, w_in, b_forget, b_branch_gate, w_ret_o, w_fox_o, w_out, ln1_g, ln1_b,
           w_peer_q, peer_sub_keys, peer_u, peer_v, w_ple_gate, b_ple_gate, w_ple, ln2_g, ln2_b):
    batch, seq, d = x.shape
    depth = w_in.shape[0]
    assert depth == 1 and d == D_MODEL, "the trunk-entry norm is fused into the single layer"
    t = batch * seq
    alpha = (2.0 * depth) ** 0.25
    fox_tq = min(1024, seq)
    row = lambda v: v.reshape(1, -1).astype(F32)
    cos_t, sin_t, intra, qdec, kdec, chunk_decay = _retention_tables(seq)

    x2 = x.reshape(t, d)
    ge, be = row(ln_emb_g), row(ln_emb_b)
    w_main, w_ff = _pack_w_in(w_in[0])
    bf_pad = jnp.zeros((1, FF_IN), F32).at[0, 0:FOX_HEADS].set(b_forget[0].astype(F32))
    proj, ff = _in_proj(x2, ge, be, w_main, w_ff)
    yr = _retention(proj, cos_t, sin_t, intra, qdec, kdec, chunk_decay, batch, seq)
    c, ct = _fcumsum(ff, bf_pad, batch, seq, fox_tq)
    yf = _fox(proj, c, ct, batch, seq, fox_tq)
    base, ht = _mix(x2, yr, yf, proj, p[0].reshape(t, PLE_DIM), ge, be, b_branch_gate[0].astype(F32),
                    w_ret_o[0].astype(BF16), w_fox_o[0].astype(BF16), w_out[0].astype(BF16),
                    row(ln1_g[0]), row(ln1_b[0]), w_ple_gate[0].astype(BF16), row(b_ple_gate[0]),
                    w_ple[0].astype(BF16), alpha)
    keys = peer_sub_keys[0].reshape(2 * PEER_HEADS, PEER_NKEYS, PEER_HALF).astype(BF16)
    ct_r, w0, r1, w1 = _route(ht, w_peer_q[0].T.astype(BF16), keys)
    out = _peer(ht, peer_u[0].astype(BF16), peer_v[0].T.astype(BF16), ct_r, w0, r1, w1, base,
                row(ln2_g[0]), row(ln2_b[0]))
    return out.reshape(batch, seq, d)
```

```python
import functools
import math

import jax
import jax.numpy as jnp
from jax import lax
from jax.experimental import pallas as pl
from jax.experimental.pallas import tpu as pltpu

F32 = jnp.float32
BF16 = jnp.bfloat16

D_MODEL = 1024
RET_HEADS = 4
RET_DK = 128
RET_DV = 256
RET_CHUNK = 128
FOX_HEADS = 8
FOX_DH = 64
PEER_HEADS = 8
PEER_NKEYS = 128
PEER_NEXP = PEER_NKEYS * PEER_NKEYS
PEER_HALF = 128
PEER_TOPK = 16
PLE_DIM = 256
LN_EPS = 1e-5
ROPE_BASE = 10000.0

RET_QK_W = RET_HEADS * RET_DK
RET_V_W = RET_HEADS * RET_DV
FOX_W = FOX_HEADS * FOX_DH

OFF_RQ, OFF_RK, OFF_RV, OFF_RG = 0, 512, 1024, 2048
OFF_GR, OFF_GF = 3072, 4096
OFF_FQ, OFF_FK, OFF_FV = 5120, 5632, 6144
PROJ_W = 6656
FF_IN = 128
FF_W = 512

V7X_VMEM_LIMIT = 56 * 1024 * 1024
NEG_BIG = -3.0e38
MARK_BASE = -(2.0 ** 127)
CAND_CELLS = tuple((a, b) for a in range(PEER_TOPK) for b in range(PEER_TOPK)
                   if (a + 1) * (b + 1) <= PEER_TOPK)


def _layer_norm(x, g, b):
    mu = jnp.mean(x, axis=-1, keepdims=True)
    xc = x - mu
    var = jnp.mean(xc * xc, axis=-1, keepdims=True)
    return xc * lax.rsqrt(var + LN_EPS) * g + b


def _params(*sem):
    return pltpu.CompilerParams(dimension_semantics=sem, vmem_limit_bytes=V7X_VMEM_LIMIT)


def _inproj_kernel(x_ref, g_ref, b_ref, w_ref, wff_ref, o_ref, ff_ref, h_scr):
    @pl.when(pl.program_id(1) == 0)
    def _():
        hb = _layer_norm(x_ref[...], g_ref[...], b_ref[...]).astype(BF16)
        h_scr[...] = hb
        ff_ref[...] = jnp.dot(hb, wff_ref[...], preferred_element_type=F32)

    o_ref[...] = jnp.dot(h_scr[...], w_ref[...], preferred_element_type=F32).astype(BF16)


def _in_proj(x2, g, b, w_main, w_ff, tm=1024, tn=3328):
    t = x2.shape[0]
    return pl.pallas_call(
        _inproj_kernel,
        grid=(t // tm, PROJ_W // tn),
        in_specs=[
            pl.BlockSpec((tm, D_MODEL), lambda i, j: (i, 0)),
            pl.BlockSpec((1, D_MODEL), lambda i, j: (0, 0)),
            pl.BlockSpec((1, D_MODEL), lambda i, j: (0, 0)),
            pl.BlockSpec((D_MODEL, tn), lambda i, j: (0, j)),
            pl.BlockSpec((D_MODEL, FF_IN), lambda i, j: (0, 0)),
        ],
        out_specs=[
            pl.BlockSpec((tm, tn), lambda i, j: (i, j)),
            pl.BlockSpec((tm, FF_IN), lambda i, j: (i, 0)),
        ],
        out_shape=[jax.ShapeDtypeStruct((t, PROJ_W), BF16), jax.ShapeDtypeStruct((t, FF_IN), F32)],
        scratch_shapes=[pltpu.VMEM((tm, D_MODEL), BF16)],
        compiler_params=_params("parallel", "arbitrary"),
        name="in_proj",
    )(x2, g, b, w_main, w_ff)


def _retention_kernel(q_ref, k_ref, v_ref, g_ref, cos_ref, sin_ref, intra_ref, qd_ref, kd_ref,
                      o_ref, r_scr, *, chunk_decay):
    @pl.when(pl.program_id(1) == 0)
    def _():
        r_scr[...] = jnp.zeros_like(r_scr)

    for ci in range(q_ref.shape[0] // RET_CHUNK):
        rows = slice(ci * RET_CHUNK, (ci + 1) * RET_CHUNK)
        cos = cos_ref[rows, :]
        sin = sin_ref[rows, :]
        for h in range(RET_HEADS):
            q = q_ref[rows, h * RET_DK:(h + 1) * RET_DK].astype(F32)
            k = k_ref[rows, h * RET_DK:(h + 1) * RET_DK].astype(F32)
            v = v_ref[rows, h * RET_DV:(h + 1) * RET_DV]
            qr = q * cos + pltpu.roll(q, RET_DK // 2, 1) * sin
            kr = (k * cos + pltpu.roll(k, RET_DK // 2, 1) * sin) * (RET_DK ** -0.5)
            s = lax.dot_general(qr.astype(BF16), kr.astype(BF16), (((1,), (1,)), ((), ())),
                                preferred_element_type=F32) * intra_ref[h]
            inner = jnp.dot(s.astype(BF16), v, preferred_element_type=F32)
            r_old = r_scr[h]
            cross = jnp.dot((qr * qd_ref[h]).astype(BF16), r_old.astype(BF16), preferred_element_type=F32)
            kv = lax.dot_general((kr * kd_ref[h]).astype(BF16), v, (((0,), (0,)), ((), ())),
                                 preferred_element_type=F32)
            r_scr[h] = chunk_decay[h] * r_old + kv
            y = inner + cross
            mu = jnp.mean(y, axis=-1, keepdims=True)
            yc = y - mu
            var = jnp.mean(yc * yc, axis=-1, keepdims=True)
            yn = yc * lax.rsqrt(var + LN_EPS)
            gate = g_ref[rows, h * RET_DV:(h + 1) * RET_DV].astype(F32)
            o_ref[rows, h * RET_DV:(h + 1) * RET_DV] = (gate * jax.nn.sigmoid(gate) * yn).astype(BF16)


def _retention(proj, cos_t, sin_t, intra, qdec, kdec, chunk_decay, batch, seq, chunks_per_step=8):
    c = RET_CHUNK * chunks_per_step
    nc = seq // c
    row = lambda b, j: b * nc + j
    return pl.pallas_call(
        functools.partial(_retention_kernel, chunk_decay=chunk_decay),
        grid=(batch, nc),
        in_specs=[
            pl.BlockSpec((c, RET_QK_W), lambda b, j: (row(b, j), OFF_RQ // RET_QK_W)),
            pl.BlockSpec((c, RET_QK_W), lambda b, j: (row(b, j), OFF_RK // RET_QK_W)),
            pl.BlockSpec((c, RET_V_W), lambda b, j: (row(b, j), OFF_RV // RET_V_W)),
            pl.BlockSpec((c, RET_V_W), lambda b, j: (row(b, j), OFF_RG // RET_V_W)),
            pl.BlockSpec((c, RET_DK), lambda b, j: (j, 0)),
            pl.BlockSpec((c, RET_DK), lambda b, j: (j, 0)),
            pl.BlockSpec((RET_HEADS, RET_CHUNK, RET_CHUNK), lambda b, j: (0, 0, 0)),
            pl.BlockSpec((RET_HEADS, RET_CHUNK, RET_DK), lambda b, j: (0, 0, 0)),
            pl.BlockSpec((RET_HEADS, RET_CHUNK, RET_DK), lambda b, j: (0, 0, 0)),
        ],
        out_specs=pl.BlockSpec((c, RET_V_W), lambda b, j: (row(b, j), 0)),
        out_shape=jax.ShapeDtypeStruct((batch * seq, RET_V_W), BF16),
        scratch_shapes=[pltpu.VMEM((RET_HEADS, RET_DK, RET_DV), F32)],
        compiler_params=_params("parallel", "arbitrary"),
        name="retention",
    )(proj, proj, proj, proj, cos_t, sin_t, intra, qdec, kdec)


def _fcumsum_kernel(ff_ref, bf_ref, tri_ref, c_ref, ct_ref, *, blk):
    seq = ff_ref.shape[0]
    carry = jnp.zeros((1, FF_IN), F32)
    for i in range(seq // blk):
        lf = jax.nn.log_sigmoid(ff_ref[i * blk:(i + 1) * blk, :] + bf_ref[...])
        cs = jnp.dot(tri_ref[...], lf, preferred_element_type=F32,
                     precision=lax.Precision.HIGHEST) + carry
        carry = cs[blk - 1:blk, :]
        for p in range(FOX_HEADS // 2):
            pair = cs if p == 0 else pltpu.roll(cs, FF_IN - 2 * p, 1)
            c_ref[i * blk:(i + 1) * blk, p * 128:(p + 1) * 128] = pair
            kb, off = divmod(i * blk, ct_ref.shape[4])
            ct_ref[0, p, kb, :, off:off + blk] = pair.T[0:8, :]


def _fcumsum(ff, bf_pad, batch, seq, tq, blk=512):
    blk = min(blk, tq)
    tri = (lax.broadcasted_iota(jnp.int32, (blk, blk), 0)
           >= lax.broadcasted_iota(jnp.int32, (blk, blk), 1)).astype(F32)
    nb = seq // tq
    return pl.pallas_call(
        functools.partial(_fcumsum_kernel, blk=blk),
        grid=(batch,),
        in_specs=[
            pl.BlockSpec((seq, FF_IN), lambda b: (b, 0)),
            pl.BlockSpec((1, FF_IN), lambda b: (0, 0)),
            pl.BlockSpec((blk, blk), lambda b: (0, 0)),
        ],
        out_specs=[
            pl.BlockSpec((seq, FF_W), lambda b: (b, 0)),
            pl.BlockSpec((1, FOX_HEADS // 2, nb, 8, tq), lambda b: (b, 0, 0, 0, 0)),
        ],
        out_shape=[jax.ShapeDtypeStruct((batch * seq, FF_W), F32),
                   jax.ShapeDtypeStruct((batch, FOX_HEADS // 2, nb, 8, tq), F32)],
        compiler_params=_params("parallel"),
        name="fcumsum",
    )(ff, bf_pad, tri)


def _fox_kernel(q_ref, k_ref, v_ref, c_ref, ct_ref, o_ref, vt_scr, ckb_scr, m_scr, l_scr, acc_scr, *, tq):
    qi = pl.program_id(2)
    seq = k_ref.shape[0]
    nkv = seq // tq

    @pl.when(qi == 0)
    def _():
        for j in range(nkv):
            vt_scr[j] = v_ref[j * tq:(j + 1) * tq, :].astype(F32).T.astype(BF16)
        for r in range(2):
            ckb_scr[r] = jnp.broadcast_to(c_ref[:, r:r + 1], (seq, 128))

    lane = lax.broadcasted_iota(jnp.int32, (tq, 128), 1)
    key_i = lax.broadcasted_iota(jnp.int32, (tq, tq), 0)
    qry_i = lax.broadcasted_iota(jnp.int32, (tq, tq), 1)
    q = q_ref[...] * jnp.asarray(FOX_DH ** -0.5, BF16)
    qh = [jnp.where((lane >= r * FOX_DH) & (lane < (r + 1) * FOX_DH), q, jnp.zeros_like(q)) for r in range(2)]
    cq = [ct_ref[0, 0, qi, r:r + 1, :] for r in range(2)]
    m_scr[...] = jnp.full_like(m_scr, NEG_BIG)
    l_scr[...] = jnp.zeros_like(l_scr)
    acc_scr[...] = jnp.zeros_like(acc_scr)

    def block(j, diag):
        start = pl.multiple_of(j * tq, tq)
        kb = k_ref[pl.ds(start, tq), :]
        vtb = vt_scr[j]
        s = [lax.dot_general(kb, qh[r], (((1,), (1,)), ((), ())), preferred_element_type=F32)
             for r in range(2)]
        ck = [ckb_scr[r, pl.ds(start, tq), :] for r in range(2)]
        s = [s[r] - jnp.concatenate([ck[r]] * (tq // 128), axis=1) for r in range(2)]
        if diag:
            s = [jnp.where(key_i <= qry_i, s[r], -1e30) for r in range(2)]
        m_old = [m_scr[r] for r in range(2)]
        m_new = [jnp.maximum(m_old[r], jnp.max(s[r], axis=0, keepdims=True) + cq[r]) for r in range(2)]
        alpha = [jnp.exp(m_old[r] - m_new[r]) for r in range(2)]
        p = [jnp.exp(s[r] + (cq[r] - m_new[r])) for r in range(2)]
        for r in range(2):
            l_scr[r] = alpha[r] * l_scr[r] + jnp.sum(p[r], axis=0, keepdims=True)
            pv = jnp.dot(vtb[r * FOX_DH:(r + 1) * FOX_DH, :], p[r].astype(BF16), preferred_element_type=F32)
            acc_scr[r] = alpha[r] * acc_scr[r] + pv
            m_scr[r] = m_new[r]

    def body(j, carry):
        block(j, False)
        return carry

    lax.fori_loop(0, qi, body, 0)
    block(qi, True)
    out_t = jnp.concatenate([acc_scr[r] / l_scr[r] for r in range(2)], axis=0)
    o_ref[...] = out_t.T.astype(BF16)


def _fox(proj, c, ct, batch, seq, tq):
    nq = seq // tq
    np_ = FOX_HEADS // 2
    return pl.pallas_call(
        functools.partial(_fox_kernel, tq=tq),
        grid=(batch, np_, nq),
        in_specs=[
            pl.BlockSpec((tq, 128), lambda b, p, i: (b * nq + i, OFF_FQ // 128 + p)),
            pl.BlockSpec((seq, 128), lambda b, p, i: (b, OFF_FK // 128 + p)),
            pl.BlockSpec((seq, 128), lambda b, p, i: (b, OFF_FV // 128 + p)),
            pl.BlockSpec((seq, 128), lambda b, p, i: (b, p)),
            pl.BlockSpec((1, 1, nq, 8, tq), lambda b, p, i: (b, p, 0, 0, 0)),
        ],
        out_specs=pl.BlockSpec((tq, 128), lambda b, p, i: (b * nq + i, p)),
        out_shape=jax.ShapeDtypeStruct((batch * seq, FOX_W), BF16),
        scratch_shapes=[
            pltpu.VMEM((nq, 128, tq), BF16),
            pltpu.VMEM((2, seq, 128), F32),
            pltpu.VMEM((2, 1, tq), F32),
            pltpu.VMEM((2, 1, tq), F32),
            pltpu.VMEM((2, FOX_DH, tq), F32),
        ],
        compiler_params=_params("parallel", "parallel", "arbitrary"),
        name="fox",
    )(proj, proj, proj, c, ct)


def _mix_kernel(x_ref, yr_ref, yf_ref, gr_ref, gf_ref, p_ref, ge_ref, be_ref, bg_ref, wro_ref, wfo_ref,
                wout_ref, g1_ref, b1_ref, wpg_ref, bpg_ref, wple_ref, base_ref, ht_ref, *, alpha):
    h0 = _layer_norm(x_ref[...], ge_ref[...], be_ref[...])
    y_ret = jnp.dot(yr_ref[...], wro_ref[...], preferred_element_type=F32)
    y_fox = jnp.dot(yf_ref[...], wfo_ref[...], preferred_element_type=F32)
    merged = (jax.nn.sigmoid(gr_ref[...].astype(F32) + bg_ref[0:1, :]) * y_ret
              + jax.nn.sigmoid(gf_ref[...].astype(F32) + bg_ref[1:2, :]) * y_fox)
    out = jnp.dot(merged.astype(BF16), wout_ref[...], preferred_element_type=F32)
    h1 = _layer_norm(alpha * h0 + out, g1_ref[...], b1_ref[...])
    h1b = h1.astype(BF16)
    gate = jax.nn.sigmoid(jnp.dot(h1b, wpg_ref[...], preferred_element_type=F32) + bpg_ref[...])
    ple = gate * jnp.dot(p_ref[...].astype(BF16), wple_ref[...], preferred_element_type=F32)
    base_ref[...] = alpha * h1 + ple
    ht_ref[0] = h1.T.astype(BF16)


def _mix(x2, yr, yf, proj, p2, ge, be, bg, wro, wfo, wout, g1, b1, wpg, bpg, wple, alpha, tm=512):
    t = x2.shape[0]
    full = lambda shape: pl.BlockSpec(shape, lambda i: (0,) * len(shape))
    return pl.pallas_call(
        functools.partial(_mix_kernel, alpha=alpha),
        grid=(t // tm,),
        in_specs=[
            pl.BlockSpec((tm, D_MODEL), lambda i: (i, 0)),
            pl.BlockSpec((tm, RET_V_W), lambda i: (i, 0)),
            pl.BlockSpec((tm, FOX_W), lambda i: (i, 0)),
            pl.BlockSpec((tm, D_MODEL), lambda i: (i, OFF_GR // D_MODEL)),
            pl.BlockSpec((tm, D_MODEL), lambda i: (i, OFF_GF // D_MODEL)),
            pl.BlockSpec((tm, PLE_DIM), lambda i: (i, 0)),
            full((1, D_MODEL)), full((1, D_MODEL)), full((2, D_MODEL)),
            full((RET_V_W, D_MODEL)), full((FOX_W, D_MODEL)), full((D_MODEL, D_MODEL)),
            full((1, D_MODEL)), full((1, D_MODEL)),
            full((D_MODEL, D_MODEL)), full((1, D_MODEL)), full((PLE_DIM, D_MODEL)),
        ],
        out_specs=[
            pl.BlockSpec((tm, D_MODEL), lambda i: (i, 0)),
            pl.BlockSpec((1, D_MODEL, tm), lambda i: (i, 0, 0)),
        ],
        out_shape=[jax.ShapeDtypeStruct((t, D_MODEL), F32),
                   jax.ShapeDtypeStruct((t // tm, D_MODEL, tm), BF16)],
        compiler_params=_params("parallel"),
        name="mix",
    )(x2, yr, yf, proj, proj, p2, ge, be, bg, wro, wfo, wout, g1, b1, wpg, bpg, wple)


def _top16(sc):
    nk = sc.shape[0]
    iota = lax.broadcasted_iota(jnp.int32, sc.shape, 0).astype(F32)
    rank = jnp.full(sc.shape, float(PEER_TOPK), F32)
    vals = []
    for a in range(PEER_TOPK):
        m = jnp.max(sc, axis=0, keepdims=True)
        first = jnp.min(jnp.where(sc == m, iota, float(nk)), axis=0, keepdims=True)
        hit = iota == first
        sc = jnp.where(hit, NEG_BIG, sc)
        rank = jnp.where(hit, float(a), rank)
        vals.append(m)
    return vals, rank


def _mark(a):
    return MARK_BASE * (1.0 + (a + 1) / 64.0)


def _top16_no_ties(chains):
    vals = [[] for _ in chains]
    for a in range(PEER_TOPK):
        ms = [jnp.max(sc, axis=0, keepdims=True) for sc in chains]
        chains = [jnp.where(sc == m, _mark(a), sc) for sc, m in zip(chains, ms)]
        for v, m in zip(vals, ms):
            v.append(m)
    return vals, chains


def _route_kernel(ht_ref, wq_ref, keys_ref, ct_ref, w0_ref, r1_ref, w1_ref,
                  q_scr, sc_scr, rank_scr, val_scr, cnt_scr, z_scr, *, lanes):
    tb = ht_ref.shape[2]
    groups = [slice(g * lanes, (g + 1) * lanes) for g in range(tb // lanes)]
    q_scr[...] = jnp.dot(wq_ref[...], ht_ref[0], preferred_element_type=F32).astype(BF16)
    val_scr[...] = jnp.zeros_like(val_scr)

    for hc in range(2 * PEER_HEADS):
        sc_scr[hc] = jnp.dot(keys_ref[hc], q_scr[hc * PEER_HALF:(hc + 1) * PEER_HALF, :],
                             preferred_element_type=F32)

    def store_vals(c, h, sl, vals):
        own_row = lax.broadcasted_iota(jnp.int32, (PEER_HEADS, lanes), 0) == h
        for a in range(PEER_TOPK):
            val_scr[c, a, :, sl] = jnp.where(own_row, vals[a], val_scr[c, a, :, sl])

    def select_cells(sl, exact):
        flat = [float(a * PEER_TOPK + b) for a, b in CAND_CELLS]
        v0 = [val_scr[0, a, :, sl] for a in range(PEER_TOPK)]
        v1 = [val_scr[1, b, :, sl] for b in range(PEER_TOPK)]
        cand = [v0[a] + v1[b] for a, b in CAND_CELLS]
        if exact:
            sel = [jnp.zeros_like(v0[0]) for _ in CAND_CELLS]
            for _ in range(PEER_TOPK):
                m = functools.reduce(jnp.maximum, cand)
                first = functools.reduce(
                    jnp.minimum, [jnp.where(cv == m, fi, 1e9) for cv, fi in zip(cand, flat)])
                hits = [first == fi for fi in flat]
                cand = [jnp.where(hit, NEG_BIG, cv) for hit, cv in zip(hits, cand)]
                sel = [jnp.where(hit, 1.0, sv) for hit, sv in zip(hits, sel)]
        else:
            thr = functools.reduce(jnp.maximum, cand)
            for _ in range(PEER_TOPK - 1):
                thr = functools.reduce(jnp.maximum, [jnp.where(cv < thr, cv, NEG_BIG) for cv in cand])
            sel = [jnp.where(cv >= thr, 1.0, 0.0) for cv in cand]
        e0 = [jnp.exp(v0[a] - v0[0]) for a in range(PEER_TOPK)]
        e1 = [jnp.exp(v1[b] - v1[0]) for b in range(PEER_TOPK)]
        z = jnp.zeros_like(v0[0])
        cnt = [jnp.zeros_like(v0[0]) for _ in range(PEER_TOPK)]
        for t, (a, b) in enumerate(CAND_CELLS):
            z = z + sel[t] * (e0[a] * e1[b])
            cnt[a] = cnt[a] + sel[t]
        z_scr[:, sl] = 1.0 / z
        for a in range(PEER_TOPK):
            cnt_scr[a, :, sl] = cnt[a]
        return jnp.abs(functools.reduce(jnp.add, sel) - float(PEER_TOPK))

    def per_head(h, carry):
        bad = jnp.zeros((1, lanes), F32)
        all_vals, all_marked = _top16_no_ties(
            [sc_scr[2 * h + c, :, sl] for sl in groups for c in range(2)])
        for g, sl in enumerate(groups):
            vals, marked = all_vals[2 * g:2 * g + 2], all_marked[2 * g:2 * g + 2]
            for c in range(2):
                taken = marked[c] <= MARK_BASE
                unit = marked[c] * -(2.0 ** -100) * (2.0 ** -27)
                rank_scr[2 * h + c, :, sl] = jnp.where(taken, (unit - 1.0) * 64.0 - 1.0, float(PEER_TOPK))
                n_taken = jnp.sum(jnp.where(taken, 1.0, 0.0), axis=0, keepdims=True)
                bad = jnp.maximum(bad, jnp.abs(n_taken - float(PEER_TOPK)))
                store_vals(c, h, sl, vals[c])

        @pl.when(jnp.max(bad) > 0.0)
        def _():
            for c in range(2):
                for sl in groups:
                    vals, rank = _top16(sc_scr[2 * h + c, :, sl])
                    rank_scr[2 * h + c, :, sl] = rank
                    store_vals(c, h, sl, vals)

        return carry

    lax.fori_loop(0, PEER_HEADS, per_head, 0)

    for sl in groups:
        excess = select_cells(sl, False)

        @pl.when(jnp.max(excess) > 0.0)
        def _(sl=sl):
            select_cells(sl, True)

    for h in range(PEER_HEADS):
        s0 = sc_scr[2 * h]
        s1 = sc_scr[2 * h + 1]
        rank0 = rank_scr[2 * h]
        ct = jnp.zeros_like(s0)
        for a in range(PEER_TOPK):
            ct = jnp.where(rank0 == float(a), cnt_scr[a, h:h + 1, :], ct)
        ct_ref[0, h] = ct
        w0_ref[0, h] = jnp.exp(s0 - val_scr[0, 0, h:h + 1, :]) * (0.5 * z_scr[h:h + 1, :])
        w1_ref[0, h] = jnp.exp(s1 - val_scr[1, 0, h:h + 1, :]).astype(BF16)
        r1_ref[0, h] = rank_scr[2 * h + 1].astype(BF16)


def _route(ht, wq_t, keys, tb=256, lanes=128):
    tm = ht.shape[2]
    t = ht.shape[0] * tm
    per = tm // tb
    nq = 2 * PEER_HEADS * PEER_HALF
    routed = lambda dt: jax.ShapeDtypeStruct((t // tb, PEER_HEADS, PEER_NKEYS, tb), dt)
    rspec = pl.BlockSpec((1, PEER_HEADS, PEER_NKEYS, tb), lambda n: (n, 0, 0, 0))
    return pl.pallas_call(
        functools.partial(_route_kernel, lanes=lanes),
        grid=(t // tb,),
        in_specs=[
            pl.BlockSpec((1, D_MODEL, tb), lambda n: (n // per, 0, n % per)),
            pl.BlockSpec((nq, D_MODEL), lambda n: (0, 0)),
            pl.BlockSpec((2 * PEER_HEADS, PEER_NKEYS, PEER_HALF), lambda n: (0, 0, 0)),
        ],
        out_specs=[rspec, rspec, rspec, rspec],
        out_shape=[routed(F32), routed(F32), routed(BF16), routed(BF16)],
        scratch_shapes=[
            pltpu.VMEM((nq, tb), BF16),
            pltpu.VMEM((2 * PEER_HEADS, PEER_NKEYS, tb), F32),
            pltpu.VMEM((2 * PEER_HEADS, PEER_NKEYS, tb), F32),
            pltpu.VMEM((2, PEER_TOPK, PEER_HEADS, tb), F32),
            pltpu.VMEM((PEER_TOPK, PEER_HEADS, tb), F32),
            pltpu.VMEM((PEER_HEADS, tb), F32),
        ],
        compiler_params=_params("parallel"),
        name="route",
    )(ht, wq_t, keys)


def _peer_kernel(ht_ref, u_ref, vt_ref, ct_ref, w0_ref, r1_ref, w1_ref, base_ref, g2_ref, b2_ref,
                 o_ref, hid_scr, a_scr, acc_scr, r1_scr, w1_scr, *, eb, lanes):
    e = pl.program_id(1)
    tb = ht_ref.shape[2]
    tr = r1_ref.shape[3]
    sub = 16

    @pl.when(e == 0)
    def _():
        acc_scr[...] = jnp.zeros_like(acc_scr)
        for rb in range(tb // tr):
            r1_scr[:, :, rb * tr:(rb + 1) * tr] = r1_ref[rb]
            w1_scr[:, :, lanes + rb * tr:lanes + (rb + 1) * tr] = w1_ref[rb]

    hid_scr[...] = jnp.dot(u_ref[...], ht_ref[0], preferred_element_type=F32)
    for il in range(eb // PEER_NKEYS):
        for lg in range(tb // lanes):
            sl = slice(lg * lanes, (lg + 1) * lanes)
            sw = slice((lg + 1) * lanes, (lg + 2) * lanes)
            rb, rl = divmod(lg * lanes, tr)
            st = slice(rl, rl + lanes)
            cts = [jnp.broadcast_to(ct_ref[rb, h, il:il + 1, st], (sub, lanes)).astype(BF16)
                   for h in range(PEER_HEADS)]
            w0s = [jnp.broadcast_to(w0_ref[rb, h, il:il + 1, st], (sub, lanes)).astype(BF16)
                   for h in range(PEER_HEADS)]
            for jb in range(PEER_NKEYS // sub):
                js = slice(jb * sub, (jb + 1) * sub)
                gsum = None
                for h in range(PEER_HEADS):
                    gate0 = jnp.minimum(jnp.maximum(cts[h] - r1_scr[h, js, sl], 0.0), w0s[h])
                    term = gate0 * w1_scr[h, js, sw]
                    gsum = term if gsum is None else gsum + term
                rows = slice(il * PEER_NKEYS + jb * sub, il * PEER_NKEYS + (jb + 1) * sub)
                hv = hid_scr[rows, sl].astype(BF16)
                act = hv * (1.0 + lax.erf(hv * jnp.asarray(1.0 / math.sqrt(2.0), BF16)))
                a_scr[rows, sl] = act * gsum
    acc_scr[...] += jnp.dot(vt_ref[...], a_scr[...], preferred_element_type=F32)

    @pl.when(e == pl.num_programs(1) - 1)
    def _():
        o_ref[...] = _layer_norm(base_ref[...] + acc_scr[...].T, g2_ref[...], b2_ref[...])


def _peer(ht, u_bf, vt_bf, ct, w0, r1, w1, base, g2, b2, eb=2048, lanes=128):
    tb = ht.shape[2]
    t = ht.shape[0] * tb
    kr = eb // PEER_NKEYS
    tr = r1.shape[3]
    rspec = pl.BlockSpec((tb // tr, PEER_HEADS, PEER_NKEYS, tr), lambda n, e: (n, 0, 0, 0))
    kspec = pl.BlockSpec((tb // tr, PEER_HEADS, kr, tr), lambda n, e: (n, 0, e, 0))
    return pl.pallas_call(
        functools.partial(_peer_kernel, eb=eb, lanes=lanes),
        grid=(t // tb, PEER_NEXP // eb),
        in_specs=[
            pl.BlockSpec((1, D_MODEL, tb), lambda n, e: (n, 0, 0)),
            pl.BlockSpec((eb, D_MODEL), lambda n, e: (e, 0)),
            pl.BlockSpec((D_MODEL, eb), lambda n, e: (0, e)),
            kspec, kspec, rspec, rspec,
            pl.BlockSpec((tb, D_MODEL), lambda n, e: (n, 0)),
            pl.BlockSpec((1, D_MODEL), lambda n, e: (0, 0)),
            pl.BlockSpec((1, D_MODEL), lambda n, e: (0, 0)),
        ],
        out_specs=pl.BlockSpec((tb, D_MODEL), lambda n, e: (n, 0)),
        out_shape=jax.ShapeDtypeStruct((t, D_MODEL), F32),
        scratch_shapes=[
            pltpu.VMEM((eb, tb), F32),
            pltpu.VMEM((eb, tb), BF16),
            pltpu.VMEM((D_MODEL, tb), F32),
            pltpu.VMEM((PEER_HEADS, PEER_NKEYS, tb + lanes), BF16),
            pltpu.VMEM((PEER_HEADS, PEER_NKEYS, tb + lanes), BF16),
        ],
        compiler_params=_params("parallel", "arbitrary"),
        name="peer",
    )(ht, u_bf, vt_bf, ct, w0, r1, w1, base, g2, b2)


def _retention_tables(seq):
    half = RET_DK // 2
    inv = ROPE_BASE ** (-jnp.arange(half, dtype=F32) / half)
    ang = jnp.arange(seq, dtype=jnp.int32).astype(F32)[:, None] * inv[None, :]
    cos, sin = jnp.cos(ang), jnp.sin(ang)
    cos_t = jnp.concatenate([cos, cos], axis=-1)
    sin_t = jnp.concatenate([-sin, sin], axis=-1)
    c = RET_CHUNK
    log_g = jnp.log(1.0 - 2.0 ** (-5.0 - jnp.arange(RET_HEADS, dtype=F32)))
    idx = jnp.arange(c, dtype=F32)
    diff = idx[:, None] - idx[None, :]
    intra = jnp.where(diff >= 0, jnp.exp(log_g[:, None, None] * jnp.maximum(diff, 0.0)), 0.0)
    qdec = jnp.broadcast_to(jnp.exp(log_g[:, None] * (idx + 1.0))[..., None], (RET_HEADS, c, RET_DK))
    kdec = jnp.broadcast_to(jnp.exp(log_g[:, None] * (c - 1.0 - idx))[..., None], (RET_HEADS, c, RET_DK))
    chunk_decay = tuple(math.exp(math.log(1.0 - 2.0 ** (-5.0 - h)) * c) for h in range(RET_HEADS))
    return cos_t, sin_t, intra, qdec, kdec, chunk_decay


def _pack_w_in(w):
    rq, rk, rv, rg, fq, fk, fv, ff, gr, gf = jnp.split(
        w, [512, 1024, 2048, 3072, 3584, 4096, 4608, 4616, 5640], axis=-1)
    main = jnp.concatenate([rq, rk, rv, rg, gr, gf, fq, fk, fv], axis=-1).astype(BF16)
    ff_pad = jnp.zeros((w.shape[0], FF_IN), w.dtype).at[:, 0:FOX_HEADS].set(ff).astype(BF16)
    return main, ff_pad


def kernel(x, p, ln_emb_g, ln_emb_b, w_in, b_forget, b_branch_gate, w_ret_o, w_fox_o, w_out, ln1_g, ln1_b,
           w_peer_q, peer_sub_keys, peer_u, peer_v, w_ple_gate, b_ple_gate, w_ple, ln2_g, ln2_b):
    batch, seq, d = x.shape
    depth = w_in.shape[0]
    assert depth == 1 and d == D_MODEL, "the trunk-entry norm is fused into the single layer"
    t = batch * seq
    alpha = (2.0 * depth) ** 0.25
    fox_tq = min(1024, seq)
    row = lambda v: v.reshape(1, -1).astype(F32)
    cos_t, sin_t, intra, qdec, kdec, chunk_decay = _retention_tables(seq)

    x2 = x.reshape(t, d)
    ge, be = row(ln_emb_g), row(ln_emb_b)
    w_main, w_ff = _pack_w_in(w_in[0])
    bf_pad = jnp.zeros((1, FF_IN), F32).at[0, 0:FOX_HEADS].set(b_forget[0].astype(F32))
    proj, ff = _in_proj(x2, ge, be, w_main, w_ff)
    yr = _retention(proj, cos_t, sin_t, intra, qdec, kdec, chunk_decay, batch, seq)
    c, ct = _fcumsum(ff, bf_pad, batch, seq, fox_tq)
    yf = _fox(proj, c, ct, batch, seq, fox_tq)
    base, ht = _mix(x2, yr, yf, proj, p[0].reshape(t, PLE_DIM), ge, be, b_branch_gate[0].astype(F32),
                    w_ret_o[0].astype(BF16), w_fox_o[0].astype(BF16), w_out[0].astype(BF16),
                    row(ln1_g[0]), row(ln1_b[0]), w_ple_gate[0].astype(BF16), row(b_ple_gate[0]),
                    w_ple[0].astype(BF16), alpha)
    keys = peer_sub_keys[0].reshape(2 * PEER_HEADS, PEER_NKEYS, PEER_HALF).astype(BF16)
    ct_r, w0, r1, w1 = _route(ht, w_peer_q[0].T.astype(BF16), keys)
    out = _peer(ht, peer_u[0].astype(BF16), peer_v[0].T.astype(BF16), ct_r, w0, r1, w1, base,
                row(ln2_g[0]), row(ln2_b[0]))
    return out.reshape(batch, seq, d)
```
